```python
import jax, jax.numpy as jnp
from jax import lax
import numpy as np

D_MODEL = 1024
BATCH = 32
SEQ = 256
DEPTH = 2
DEC_BATCH = 4
DEC_SEQ = 4096
PAST_LEN = 256

GRID_W = 64
HEAD_DIM = 64
N_HEADS_TOTAL = D_MODEL // HEAD_DIM
H_NA = N_HEADS_TOTAL // 4
H_RET = N_HEADS_TOTAL // 4
H_GQA = N_HEADS_TOTAL // 2
KV_GQA = H_GQA // 4
D_NA = H_NA * HEAD_DIM
D_RET = H_RET * HEAD_DIM
D_GQA = H_GQA * HEAD_DIM
D_KV = KV_GQA * HEAD_DIM
D_MIX = D_NA + D_RET + D_GQA
IN_SIZES = (D_NA, D_NA, D_NA, D_RET, D_RET, D_RET, D_RET, D_GQA, D_KV, D_KV)
D_IN = 3 * D_NA + 4 * D_RET + D_GQA + 2 * D_KV
WIN_R = 8
WIN_C = 16
RET_CHUNK = 128
Q_BLOCK = 128
N_EXPERTS = 32
TOP_K = 4
D_FF = D_MODEL
SWIGLU_LIMIT = 7.0
SWIGLU_ALPHA = 1.702
MOE_BLOCK = 256
ROPE_THETA = 10000.0
EPS = 1e-6
ATTN_SCALE = HEAD_DIM ** -0.5

kernel_name = "hybrid_natten_retnet_gqa_moe_diffusion_step"


def rms_norm(x, g=None):
    xf = x.astype(jnp.float32)
    y = xf * lax.rsqrt(jnp.mean(xf * xf, axis=-1, keepdims=True) + EPS)
    if g is not None:
        y = y * g.astype(jnp.float32)
    return y.astype(x.dtype)


def modulation(cond, ada_w, ada_b):
    m = jax.nn.silu(cond) @ ada_w + ada_b
    return jnp.split(m[:, None, :], 6, axis=-1)


def axial_rope(L):
    t = jnp.arange(L)
    row = (t // GRID_W).astype(jnp.float32)
    col = (t % GRID_W).astype(jnp.float32)
    n = HEAD_DIM // 4
    inv = ROPE_THETA ** (-jnp.arange(n, dtype=jnp.float32) / n)
    ang = jnp.concatenate([row[:, None] * inv, col[:, None] * inv], axis=-1)
    return jnp.cos(ang), jnp.sin(ang)


def retention_rope(L):
    inv = 1.0 / (ROPE_THETA ** jnp.linspace(0.0, 1.0, HEAD_DIM // 2, dtype=jnp.float32))
    ang = jnp.arange(L, dtype=jnp.float32)[:, None] * inv
    return jnp.cos(ang), jnp.sin(ang)


def apply_rope(x, cos, sin):
    x1, x2 = jnp.split(x, 2, axis=-1)
    c = cos[None, :, None, :].astype(x.dtype)
    s = sin[None, :, None, :].astype(x.dtype)
    return jnp.concatenate([x1 * c - x2 * s, x2 * c + x1 * s], axis=-1)


def split_projection(z):
    B, L, _ = z.shape
    parts = []
    off = 0
    for size in IN_SIZES:
        parts.append(z[..., off:off + size])
        off += size
    a_q, a_k, a_v, b_q, b_k, b_v, b_g, c_q, c_k, c_v = parts
    hd = lambda t: t.reshape(B, L, -1, HEAD_DIM)
    return (hd(a_q), hd(a_k), hd(a_v), hd(b_q), hd(b_k), hd(b_v), b_g,
            hd(c_q), hd(c_k), hd(c_v))


def blocked_attention(q, k, v):
    B, Lq, Hq, d = q.shape
    Hkv = k.shape[2]
    G = Hq // Hkv
    nb = Lq // Q_BLOCK
    qb = q.reshape(B, nb, Q_BLOCK, Hkv, G, d).transpose(1, 0, 2, 3, 4, 5)

    def one_block(qi):
        s = jnp.einsum('bqhgd,bkhd->bhgqk', qi, k).astype(jnp.float32) * ATTN_SCALE
        p = jax.nn.softmax(s, axis=-1).astype(v.dtype)
        return jnp.einsum('bhgqk,bkhd->bqhgd', p, v)

    o = lax.map(one_block, qb)
    return o.transpose(1, 0, 2, 3, 4, 5).reshape(B, Lq, Hq, d)


def neighbourhood_attention(q, k, v, k_ctx, v_ctx, rpb):
    B, L, H, d = q.shape
    rows = L // GRID_W
    kr = min(WIN_R, rows)
    grid = lambda t: t.reshape(B, rows, GRID_W, H, d)
    qg, kg, vg = grid(q), grid(k), grid(v)
    row_start = jnp.clip(jnp.arange(rows) - kr // 2, 0, rows - kr)
    cols = jnp.arange(GRID_W)
    col_win = jnp.clip(cols - WIN_C // 2, 0, GRID_W - WIN_C)[:, None] + jnp.arange(WIN_C)
    col_bias_idx = col_win - cols[:, None] + (WIN_C - 1)
    n_loc = kr * WIN_C

    def one_row(args):
        r, qr = args
        rs = row_start[r]
        kband = lax.dynamic_slice_in_dim(kg, rs, kr, axis=1)
        vband = lax.dynamic_slice_in_dim(vg, rs, kr, axis=1)
        kwin = kband[:, :, col_win]
        vwin = vband[:, :, col_win]
        row_bias_idx = rs + jnp.arange(kr) - r + (WIN_R - 1)
        bias = rpb[:, row_bias_idx[:, None, None], col_bias_idx[None]]
        s_loc = (jnp.einsum('bqhd,biqjhd->bhqij', qr, kwin).astype(jnp.float32) * ATTN_SCALE
                 + jnp.transpose(bias, (0, 2, 1, 3)).astype(jnp.float32)[None])
        s_ctx = jnp.einsum('bqhd,bkhd->bhqk', qr, k_ctx).astype(jnp.float32) * ATTN_SCALE
        s = jnp.concatenate([s_loc.reshape(B, H, GRID_W, n_loc), s_ctx], axis=-1)
        p = jax.nn.softmax(s, axis=-1).astype(v.dtype)
        p_loc = p[..., :n_loc].reshape(B, H, GRID_W, kr, WIN_C)
        return (jnp.einsum('bhqij,biqjhd->bqhd', p_loc, vwin)
                + jnp.einsum('bhqk,bkhd->bqhd', p[..., n_loc:], v_ctx))

    o = lax.map(one_row, (jnp.arange(rows), jnp.moveaxis(qg, 1, 0)))
    return jnp.moveaxis(o, 0, 1).reshape(B, L, H, d)


def retention_scan(q, k, v, log_gamma, s0):
    B, L, H, d = q.shape
    C = RET_CHUNK
    n = L // C
    chunk = lambda t: t.astype(jnp.float32).reshape(B, n, C, H, d).transpose(1, 0, 3, 2, 4)
    qc, kc, vc = chunk(q), chunk(k), chunk(v)
    idx = jnp.arange(C, dtype=jnp.float32)
    diff = idx[:, None] - idx[None, :]
    lg = log_gamma[:, None, None]
    decay_mask = jnp.where(diff >= 0, jnp.exp(lg * jnp.maximum(diff, 0.0)), 0.0)
    q_decay = jnp.exp(log_gamma[:, None] * (idx + 1.0))[None, :, :, None]
    k_decay = jnp.exp(log_gamma[:, None] * (C - 1.0 - idx))[None, :, :, None]
    chunk_decay = jnp.exp(log_gamma * C)[None, :, None, None]

    def step(s, inp):
        qi, ki, vi = inp
        inner = jnp.einsum('bhnm,bhmd->bhnd', jnp.einsum('bhnd,bhmd->bhnm', qi, ki) * decay_mask, vi)
        cross = jnp.einsum('bhnd,bhde->bhne', qi, s) * q_decay
        s_new = s * chunk_decay + jnp.einsum('bhmd,bhme->bhde', ki * k_decay, vi)
        return s_new, inner + cross

    s_fin, out = lax.scan(step, s0.astype(jnp.float32), (qc, kc, vc))
    return out.transpose(1, 0, 3, 2, 4).reshape(B, L, H, d), s_fin


def retention_bidir(q, k, v, decay_logit, s0_f, s0_b):
    log_g = jax.nn.log_sigmoid(decay_logit.astype(jnp.float32))
    flip = lambda t: t[:, ::-1]
    o_f, s_f = retention_scan(q, k, v, log_g[0], s0_f)
    o_b, s_b = retention_scan(flip(q), flip(k), flip(v), log_g[1], s0_b)
    return o_f + flip(o_b), s_f, s_b


def merge_head_groups(o_a, o_b, b_g, o_c, w_out):
    B, L = o_a.shape[:2]
    dt = o_a.dtype
    ob = rms_norm(o_b).reshape(B, L, D_RET).astype(dt) * jax.nn.silu(b_g)
    z = jnp.concatenate([o_a.reshape(B, L, D_NA), ob, o_c.reshape(B, L, D_GQA)], axis=-1)
    return z @ w_out


def context_mixers(h, mp):
    B = h.shape[0]
    a_q, a_k, a_v, b_q, b_k, b_v, b_g, c_q, c_k, c_v = split_projection(h @ mp['w_in'])
    o_a = blocked_attention(a_q, a_k, a_v)
    zero = jnp.zeros((B, H_RET, HEAD_DIM, HEAD_DIM), jnp.float32)
    o_b, s_f, s_b = retention_bidir(b_q, b_k * ATTN_SCALE, b_v, mp['decay'], zero, zero)
    c_q = rms_norm(c_q, mp['q_g'])
    c_k = rms_norm(c_k, mp['k_g'])
    o_c = blocked_attention(c_q, c_k, c_v)
    o = merge_head_groups(o_a, o_b, b_g, o_c, mp['w_out'])
    state = jnp.stack([s_f, s_b], axis=1).astype(h.dtype)
    return o, (a_k, a_v, c_k, c_v, state)


def latent_mixers(h, mp, ctx_na_k, ctx_na_v, ctx_gqa_k, ctx_gqa_v, ctx_state):
    L = h.shape[1]
    a_q, a_k, a_v, b_q, b_k, b_v, b_g, c_q, c_k, c_v = split_projection(h @ mp['w_in'])
    o_a = neighbourhood_attention(a_q, a_k, a_v, ctx_na_k, ctx_na_v, mp['rpb'])
    cos1, sin1 = retention_rope(L)
    o_b, _, _ = retention_bidir(apply_rope(b_q, cos1, sin1), apply_rope(b_k, cos1, sin1) * ATTN_SCALE,
                                b_v, mp['decay'], ctx_state[:, 0], ctx_state[:, 1])
    cos2, sin2 = axial_rope(L)
    c_q = apply_rope(rms_norm(c_q, mp['q_g']), cos2, sin2)
    c_k = apply_rope(rms_norm(c_k, mp['k_g']), cos2, sin2)
    o_c = blocked_attention(c_q, jnp.concatenate([ctx_gqa_k, c_k], axis=1),
                            jnp.concatenate([ctx_gqa_v, c_v], axis=1))
    return merge_head_groups(o_a, o_b, b_g, o_c, mp['w_out'])


def moe_ffn(x, router_w, router_b, w_gate_up, b_gate_up, w_down, b_down):
    N, D = x.shape
    NK = N * TOP_K
    logits = (x @ router_w + router_b).astype(jnp.float32)
    top_logits, top_idx = lax.top_k(logits, TOP_K)
    gates = jax.nn.softmax(top_logits, axis=-1)
    flat_e = top_idx.reshape(-1)
    order = jnp.argsort(flat_e)
    sorted_e = flat_e[order]
    sorted_tok = (order // TOP_K).astype(jnp.int32)
    counts = jnp.bincount(flat_e, length=N_EXPERTS)
    padded = ((counts + MOE_BLOCK - 1) // MOE_BLOCK) * MOE_BLOCK
    pad_end = jnp.cumsum(padded)
    pad_start = pad_end - padded
    start = jnp.cumsum(counts) - counts
    dest = pad_start[sorted_e] + jnp.arange(NK) - start[sorted_e]
    n_blocks = -(-NK // MOE_BLOCK) + N_EXPERTS
    P = n_blocks * MOE_BLOCK
    slot_tok = jnp.full((P,), N, jnp.int32).at[dest].set(sorted_tok)
    slot_gate = jnp.zeros((P,), jnp.float32).at[dest].set(gates.reshape(-1)[order])
    block_e = jnp.minimum(jnp.searchsorted(pad_end, jnp.arange(n_blocks) * MOE_BLOCK, side='right'),
                          N_EXPERTS - 1)
    x_pad = jnp.concatenate([x, jnp.zeros((1, D), x.dtype)], axis=0)
    xb = x_pad[slot_tok].reshape(n_blocks, MOE_BLOCK, D)

    def expert_block(args):
        xi, e = args
        gu = xi @ w_gate_up[e] + b_gate_up[e]
        x_glu = jnp.minimum(gu[:, 0::2], SWIGLU_LIMIT)
        x_lin = jnp.clip(gu[:, 1::2], -SWIGLU_LIMIT, SWIGLU_LIMIT)
        act = x_glu * jax.nn.sigmoid(SWIGLU_ALPHA * x_glu) * (x_lin + 1.0)
        return act @ w_down[e] + b_down[e]

    yb = lax.map(expert_block, (xb, block_e)).reshape(P, D)
    y = jax.ops.segment_sum(yb.astype(jnp.float32) * slot_gate[:, None], slot_tok, num_segments=N + 1)
    return y[:N].astype(x.dtype)


def ffn_sublayer(x, shift, scale, gate, g, fp):
    B, L, D = x.shape
    h = rms_norm(x, g) * (1.0 + scale) + shift
    return x + gate * moe_ffn(h.reshape(B * L, D), *fp).reshape(B, L, D)


def setup_inputs(seed: int = 0) -> dict:
    key = jax.random.key(seed)
    ks = jax.random.split(key, 26)
    nrm = lambda k, shape, s=1.0: s * jax.random.normal(k, shape, jnp.float32)
    gain = lambda k, shape: 1.0 + 0.02 * jax.random.normal(k, shape, jnp.float32)
    a = 5.0 + np.arange(H_RET, dtype=np.float32)
    base_logit = jnp.asarray(np.log(2.0 ** a - 1.0).astype(np.float32))
    return {
        "x_prompt": nrm(ks[0], (BATCH, SEQ, D_MODEL)),
        "x_sample": nrm(ks[1], (DEC_BATCH, DEC_SEQ, D_MODEL)),
        "cache_na_k": nrm(ks[2], (DEC_BATCH, DEPTH, PAST_LEN, H_NA, HEAD_DIM)),
        "cache_na_v": nrm(ks[3], (DEC_BATCH, DEPTH, PAST_LEN, H_NA, HEAD_DIM)),
        "cache_gqa_k": nrm(ks[4], (DEC_BATCH, DEPTH, PAST_LEN, KV_GQA, HEAD_DIM)),
        "cache_gqa_v": nrm(ks[5], (DEC_BATCH, DEPTH, PAST_LEN, KV_GQA, HEAD_DIM)),
        "state_ret": nrm(ks[6], (DEC_BATCH, DEPTH, 2, H_RET, HEAD_DIM, HEAD_DIM), 0.5),
        "c": nrm(ks[7], (DEC_BATCH, D_MODEL)),
        "c_ctx": nrm(ks[8], (D_MODEL,)),
        "ada_w": nrm(ks[9], (DEPTH, D_MODEL, 6 * D_MODEL), 0.5 * D_MODEL ** -0.5),
        "ada_b": nrm(ks[10], (DEPTH, 6 * D_MODEL), 0.02),
        "norm1_g": gain(ks[11], (DEPTH, D_MODEL)),
        "norm2_g": gain(ks[12], (DEPTH, D_MODEL)),
        "w_in": nrm(ks[13], (DEPTH, D_MODEL, D_IN), D_MODEL ** -0.5),
        "w_out": nrm(ks[14], (DEPTH, D_MIX, D_MODEL), D_MIX ** -0.5),
        "na_rpb": nrm(ks[15], (DEPTH, H_NA, 2 * WIN_R - 1, 2 * WIN_C - 1), 0.1),
        "ret_decay_logit": base_logit[None, None, :] + nrm(ks[16], (DEPTH, 2, H_RET), 0.1),
        "q_norm_g": gain(ks[17], (DEPTH, HEAD_DIM)),
        "k_norm_g": gain(ks[18], (DEPTH, HEAD_DIM)),
        "router_w": nrm(ks[19], (DEPTH, D_MODEL, N_EXPERTS), D_MODEL ** -0.5),
        "router_b": nrm(ks[20], (DEPTH, N_EXPERTS), 0.01),
        "w_gate_up": nrm(ks[21], (DEPTH, N_EXPERTS, D_MODEL, 2 * D_FF), D_MODEL ** -0.5),
        "b_gate_up": nrm(ks[22], (DEPTH, N_EXPERTS, 2 * D_FF), 0.01),
        "w_down": nrm(ks[23], (DEPTH, N_EXPERTS, D_FF, D_MODEL), D_FF ** -0.5),
        "b_down": nrm(ks[24], (DEPTH, N_EXPERTS, D_MODEL), 0.01),
        "final_g": gain(ks[25], (D_MODEL,)),
    }


def reference(x_prompt, x_sample, cache_na_k, cache_na_v, cache_gqa_k, cache_gqa_v, state_ret, c, c_ctx,
              ada_w, ada_b, norm1_g, norm2_g, w_in, w_out, na_rpb, ret_decay_logit, q_norm_g, k_norm_g,
              router_w, router_b, w_gate_up, b_gate_up, w_down, b_down, final_g):
    xp, xs = x_prompt, x_sample
    new_na_k, new_na_v, new_gqa_k, new_gqa_v, new_ret = [], [], [], [], []
    for l in range(DEPTH):
        mp = {"w_in": w_in[l], "w_out": w_out[l], "rpb": na_rpb[l], "decay": ret_decay_logit[l],
              "q_g": q_norm_g[l], "k_g": k_norm_g[l]}
        fp = (router_w[l], router_b[l], w_gate_up[l], b_gate_up[l], w_down[l], b_down[l])
        sh1, sc1, g1, sh2, sc2, g2 = modulation(c_ctx[None, :], ada_w[l], ada_b[l])
        o, (ak, av, gk, gv, st) = context_mixers(rms_norm(xp, norm1_g[l]) * (1.0 + sc1) + sh1, mp)
        xp = xp + g1 * o
        xp = ffn_sublayer(xp, sh2, sc2, g2, norm2_g[l], fp)
        new_na_k.append(ak)
        new_na_v.append(av)
        new_gqa_k.append(gk)
        new_gqa_v.append(gv)
        new_ret.append(st)
        sh1, sc1, g1, sh2, sc2, g2 = modulation(c, ada_w[l], ada_b[l])
        o = latent_mixers(rms_norm(xs, norm1_g[l]) * (1.0 + sc1) + sh1, mp,
                          cache_na_k[:, l], cache_na_v[:, l], cache_gqa_k[:, l], cache_gqa_v[:, l],
                          state_ret[:, l])
        xs = xs + g1 * o
        xs = ffn_sublayer(xs, sh2, sc2, g2, norm2_g[l], fp)
    y_prompt = rms_norm(xp, final_g)
    y_sample = rms_norm(xs, final_g)
    return (y_prompt, y_sample, jnp.stack(new_na_k, axis=1), jnp.stack(new_na_v, axis=1),
            jnp.stack(new_gqa_k, axis=1), jnp.stack(new_gqa_v, axis=1), jnp.stack(new_ret, axis=1))
```

```python
import functools

import jax
import jax.numpy as jnp
import numpy as np
from jax import lax
from jax.experimental import pallas as pl
from jax.experimental.pallas import tpu as pltpu

F32 = jnp.float32
BF16 = jnp.bfloat16

D_MODEL = 1024
BATCH = 32
SEQ = 256
DEPTH = 2
DEC_BATCH = 4
DEC_SEQ = 4096
PAST_LEN = 256
GRID_W = 64
HEAD_DIM = 64
H_NA = 4
H_RET = 4
H_GQA = 8
KV_GQA = 2
D_NA = H_NA * HEAD_DIM
D_RET = H_RET * HEAD_DIM
D_GQA = H_GQA * HEAD_DIM
D_KV = KV_GQA * HEAD_DIM
D_IN = 3 * D_NA + 4 * D_RET + D_GQA + 2 * D_KV
WIN_R = 8
WIN_C = 16
RET_CHUNK = 128
N_EXPERTS = 32
TOP_K = 4
D_FF = D_MODEL
SWIGLU_LIMIT = 7.0
SWIGLU_ALPHA = 1.702
MOE_BLOCK = 256
ROPE_THETA = 10000.0
EPS = 1e-6
ATTN_SCALE = HEAD_DIM ** -0.5

NCTX = BATCH * SEQ
NLAT = DEC_BATCH * DEC_SEQ
NTOK = NCTX + NLAT
N_COND = 8
NK = NTOK * TOP_K
N_MOE_BLOCKS = NK // MOE_BLOCK + N_EXPERTS
LANES = 128
NEG_BIG = -1e30

C_AQ, C_AK, C_AV = 0, D_NA, 2 * D_NA
C_BQ = 3 * D_NA
C_BK = C_BQ + D_RET
C_BV = C_BK + D_RET
C_BG = C_BV + D_RET
C_CQ = C_BG + D_RET
C_CK = C_CQ + D_GQA
C_CV = C_CK + D_KV

ROW_TILE = 512
NA_QROWS = 4
NA_BAND = 12
GQA_TQ = 128
VMEM_LIMIT = 56 * 1024 * 1024


def _cparams(n_axes):
    return pltpu.CompilerParams(dimension_semantics=("arbitrary",) * n_axes,
                                vmem_limit_bytes=VMEM_LIMIT)


def _mod_index(i, tile):
    nctx = NCTX // tile
    per_batch = DEC_SEQ // tile
    return jnp.where(i < nctx, 0, 1 + (i - nctx) // per_batch)


def _dot(a, b):
    return jnp.dot(a, b, preferred_element_type=F32)


def _dot_nt(a, b):
    return lax.dot_general(a, b, (((1,), (1,)), ((), ())), preferred_element_type=F32)


def _dot_tn(a, b):
    return lax.dot_general(a, b, (((0,), (0,)), ((), ())), preferred_element_type=F32)


def _mod_kernel(c_ref, w_ref, b_ref, o_ref):
    c = c_ref[...]
    s = c * jax.nn.sigmoid(c)
    o_ref[...] = jnp.dot(s, w_ref[...], preferred_element_type=F32,
                         precision=lax.Precision.HIGHEST) + b_ref[...]


def _modulation(cond, ada_w, ada_b):
    tn = 1536
    return pl.pallas_call(
        _mod_kernel,
        grid=(DEPTH, 6 * D_MODEL // tn),
        in_specs=[pl.BlockSpec((N_COND, D_MODEL), lambda l, j: (0, 0)),
                  pl.BlockSpec((None, D_MODEL, tn), lambda l, j: (l, 0, j)),
                  pl.BlockSpec((None, 1, tn), lambda l, j: (l, 0, j))],
        out_specs=pl.BlockSpec((None, N_COND, tn), lambda l, j: (l, 0, j)),
        out_shape=jax.ShapeDtypeStruct((DEPTH, N_COND, 6 * D_MODEL), F32),
        compiler_params=_cparams(2),
    )(cond, ada_w, ada_b.reshape(DEPTH, 1, 6 * D_MODEL))


def _rms_mod(x, g, sc, sh):
    ms = jnp.mean(x * x, axis=-1, keepdims=True)
    return (x * lax.rsqrt(ms + EPS) * g) * (1.0 + sc) + sh


def _two_head_rsqrt(x):
    sq = x * x
    left = lax.broadcasted_iota(jnp.int32, x.shape, 1) < HEAD_DIM
    s_left = jnp.sum(jnp.where(left, sq, 0.0), axis=-1, keepdims=True)
    s_right = jnp.sum(jnp.where(left, 0.0, sq), axis=-1, keepdims=True)
    ms = jnp.where(left, s_left, s_right) * (1.0 / HEAD_DIM)
    return lax.rsqrt(ms + EPS)


def _rope128(x, cos, sin_signed):
    first = (lax.broadcasted_iota(jnp.int32, x.shape, 1) & (HEAD_DIM // 2)) == 0
    rot = jnp.where(first, pltpu.roll(x, LANES - HEAD_DIM // 2, 1), pltpu.roll(x, HEAD_DIM // 2, 1))
    return x * cos + rot * sin_signed


def _softmax_pv(scores, values):
    m = functools.reduce(jnp.maximum, [jnp.max(s, axis=-1, keepdims=True) for s in scores])
    ps = [jnp.exp(s - m) for s in scores]
    denom = functools.reduce(lambda a, b: a + b, [jnp.sum(p, axis=-1, keepdims=True) for p in ps])
    o = functools.reduce(lambda a, b: a + b, [_dot(p.astype(BF16), v) for p, v in zip(ps, values)])
    return o / denom


def _head(x, h):
    return x[:, h * HEAD_DIM:(h + 1) * HEAD_DIM]


def _in_proj_kernel(x_ref, g_ref, sc_ref, sh_ref, w_ref, rc_ref, rs_ref, ac_ref, as_ref,
                    qg_ref, kg_ref, z_ref):
    hb = _rms_mod(x_ref[...], g_ref[...], sc_ref[...], sh_ref[...]).astype(BF16)

    def proj(c0, c1):
        return _dot(hb, w_ref[:, c0:c1])

    z_ref[:, C_AQ:C_AK] = proj(C_AQ, C_AK) * ATTN_SCALE
    z_ref[:, C_AK:C_BQ] = proj(C_AK, C_BQ)
    rc, rs = rc_ref[...], rs_ref[...]
    zq = proj(C_BQ, C_BK)
    zk = proj(C_BK, C_BV)
    for j in range(D_RET // LANES):
        sl = slice(j * LANES, (j + 1) * LANES)
        z_ref[:, C_BQ + j * LANES:C_BQ + (j + 1) * LANES] = _rope128(zq[:, sl], rc, rs)
        z_ref[:, C_BK + j * LANES:C_BK + (j + 1) * LANES] = _rope128(zk[:, sl], rc, rs) * ATTN_SCALE
    z_ref[:, C_BV:C_CQ] = proj(C_BV, C_CQ)
    ac, asn = ac_ref[...], as_ref[...]
    zc = proj(C_CQ, C_CK)
    for j in range(D_GQA // LANES):
        t = zc[:, j * LANES:(j + 1) * LANES]
        t = t * _two_head_rsqrt(t) * qg_ref[...]
        z_ref[:, C_CQ + j * LANES:C_CQ + (j + 1) * LANES] = _rope128(t, ac, asn) * ATTN_SCALE
    zkv = proj(C_CK, D_IN)
    t = zkv[:, :LANES]
    t = t * _two_head_rsqrt(t) * kg_ref[...]
    z_ref[:, C_CK:C_CV] = _rope128(t, ac, asn)
    z_ref[:, C_CV:D_IN] = zkv[:, LANES:]


def _in_proj(x, g, sc, sh, w_b, tabs, qg, kg):
    tm = ROW_TILE
    nctx = NCTX // tm
    per_seq = DEC_SEQ // tm

    def tab_idx(i):
        return (jnp.where(i < nctx, per_seq, (i - nctx) % per_seq), 0)

    row = lambda i: (i, 0)
    fixed = lambda i: (0, 0)
    mod = lambda i: (_mod_index(i, tm), 0, 0)
    tab_spec = pl.BlockSpec((tm, LANES), tab_idx)
    return pl.pallas_call(
        _in_proj_kernel,
        grid=(NTOK // tm,),
        in_specs=[pl.BlockSpec((tm, D_MODEL), row),
                  pl.BlockSpec((1, D_MODEL), fixed),
                  pl.BlockSpec((None, 1, D_MODEL), mod),
                  pl.BlockSpec((None, 1, D_MODEL), mod),
                  pl.BlockSpec((D_MODEL, D_IN), fixed),
                  tab_spec, tab_spec, tab_spec, tab_spec,
                  pl.BlockSpec((1, LANES), fixed),
                  pl.BlockSpec((1, LANES), fixed)],
        out_specs=pl.BlockSpec((tm, D_IN), row),
        out_shape=jax.ShapeDtypeStruct((NTOK, D_IN), F32),
        compiler_params=_cparams(1),
    )(x, g, sc, sh, w_b, *tabs, qg, kg)


def _rope_tables():
    t = np.arange(DEC_SEQ)
    inv_ret = 1.0 / (ROPE_THETA ** np.linspace(0.0, 1.0, HEAD_DIM // 2, dtype=np.float32))
    ang_ret = t.astype(np.float32)[:, None] * inv_ret.astype(np.float32)
    n = HEAD_DIM // 4
    inv_ax = (ROPE_THETA ** (-np.arange(n, dtype=np.float32) / n)).astype(np.float32)
    row = (t // GRID_W).astype(np.float32)
    col = (t % GRID_W).astype(np.float32)
    ang_ax = np.concatenate([row[:, None] * inv_ax, col[:, None] * inv_ax], axis=-1)

    def tables(ang):
        ang = jnp.asarray(ang, F32)
        cos, sin = jnp.cos(ang), jnp.sin(ang)
        cos2 = jnp.tile(jnp.concatenate([cos, cos], axis=-1), (1, LANES // HEAD_DIM))
        sin2 = jnp.tile(jnp.concatenate([-sin, sin], axis=-1), (1, LANES // HEAD_DIM))
        cos2 = jnp.concatenate([cos2, jnp.ones((ROW_TILE, LANES), F32)], axis=0)
        sin2 = jnp.concatenate([sin2, jnp.zeros((ROW_TILE, LANES), F32)], axis=0)
        return cos2, sin2

    return (*tables(ang_ret), *tables(ang_ax))


def _ctx_attn_kernel(aq_ref, ak_ref, av_ref, cq0_ref, cq1_ref, ck_ref, cv_ref, oa_ref, oc_ref):
    aq = aq_ref[...].astype(BF16)
    ak = ak_ref[...].astype(BF16)
    av = av_ref[...].astype(BF16)
    outs = []
    for h in range(H_NA):
        s = _dot_nt(_head(aq, h), _head(ak, h))
        outs.append(_softmax_pv([s], [_head(av, h)]))
    oa_ref[...] = jnp.concatenate(outs, axis=-1)
    ck = ck_ref[...].astype(BF16)
    cv = cv_ref[...].astype(BF16)
    group = H_GQA // KV_GQA
    outs = []
    for g, cq_ref in enumerate((cq0_ref, cq1_ref)):
        cq = cq_ref[...].astype(BF16)
        qs = jnp.concatenate([_head(cq, j) for j in range(group)], axis=0)
        o = _softmax_pv([_dot_nt(qs, _head(ck, g))], [_head(cv, g)])
        outs += [o[j * SEQ:(j + 1) * SEQ] for j in range(group)]
    oc_ref[...] = jnp.concatenate(outs, axis=-1)


def _ctx_attention(z):
    col = lambda c, w: pl.BlockSpec((SEQ, w), lambda b: (b, c // w))
    return pl.pallas_call(
        _ctx_attn_kernel,
        grid=(BATCH,),
        in_specs=[col(C_AQ, D_NA), col(C_AK, D_NA), col(C_AV, D_NA),
                  col(C_CQ, D_NA), col(C_CQ + D_NA, D_NA), col(C_CK, D_KV), col(C_CV, D_KV)],
        out_specs=[pl.BlockSpec((SEQ, D_NA), lambda b: (b, 0)),
                   pl.BlockSpec((SEQ, D_GQA), lambda b: (b, 0))],
        out_shape=[jax.ShapeDtypeStruct((NCTX, D_NA), F32),
                   jax.ShapeDtypeStruct((NCTX, D_GQA), F32)],
        compiler_params=_cparams(1),
    )(z, z, z, z, z, z, z)


def _lat_gqa_kernel(cq0_ref, cq1_ref, ck_ref, cv_ref, kc_ref, vc_ref, oc_ref):
    ck = ck_ref[...].astype(BF16)
    cv = cv_ref[...].astype(BF16)
    kc = kc_ref[...].astype(BF16)
    vc = vc_ref[...].astype(BF16)
    group = H_GQA // KV_GQA
    outs = []
    for g, cq_ref in enumerate((cq0_ref, cq1_ref)):
        cq = cq_ref[...].astype(BF16)
        qs = jnp.concatenate([_head(cq, j) for j in range(group)], axis=0)
        o = _softmax_pv([_dot_nt(qs, _head(kc, g)), _dot_nt(qs, _head(ck, g))],
                        [_head(vc, g), _head(cv, g)])
        outs += [o[j * GQA_TQ:(j + 1) * GQA_TQ] for j in range(group)]
    oc_ref[...] = jnp.concatenate(outs, axis=-1)


def _lat_gqa(z, cache_k, cache_v, layer):
    nq = DEC_SEQ // GQA_TQ
    q0 = NCTX // GQA_TQ
    s0 = NCTX // DEC_SEQ
    qspec = lambda c: pl.BlockSpec((GQA_TQ, D_NA), lambda b, i: (q0 + b * nq + i, c // D_NA))
    kvspec = lambda c: pl.BlockSpec((DEC_SEQ, D_KV), lambda b, i: (s0 + b, c // D_KV))
    cspec = pl.BlockSpec((None, None, PAST_LEN, D_KV), lambda b, i: (b, layer, 0, 0))
    return pl.pallas_call(
        _lat_gqa_kernel,
        grid=(DEC_BATCH, nq),
        in_specs=[qspec(C_CQ), qspec(C_CQ + D_NA), kvspec(C_CK), kvspec(C_CV), cspec, cspec],
        out_specs=pl.BlockSpec((GQA_TQ, D_GQA), lambda b, i: (b * nq + i, 0)),
        out_shape=jax.ShapeDtypeStruct((NLAT, D_GQA), F32),
        compiler_params=_cparams(2),
    )(z, z, z, z, cache_k, cache_v)


def _na_band_start(i):
    return jnp.clip(NA_QROWS * i - WIN_R // 2, 0, GRID_W - NA_BAND)


def _lat_na_kernel(q_ref, k_ref, v_ref, kc_ref, vc_ref, bias_ref, o_ref):
    i = pl.program_id(1)
    start = pl.multiple_of(_na_band_start(i) * GRID_W, GRID_W)
    q = q_ref[...].astype(BF16)
    kb = k_ref[pl.ds(start, NA_BAND * GRID_W), :].astype(BF16)
    vb = v_ref[pl.ds(start, NA_BAND * GRID_W), :].astype(BF16)
    kc = kc_ref[...].astype(BF16)
    vc = vc_ref[...].astype(BF16)
    outs = []
    for h in range(H_NA):
        qh = _head(q, h)
        s_loc = _dot_nt(qh, _head(kb, h)) + bias_ref[h]
        s_ctx = _dot_nt(qh, _head(kc, h))
        outs.append(_softmax_pv([s_loc, s_ctx], [_head(vb, h), _head(vc, h)]))
    o_ref[...] = jnp.concatenate(outs, axis=-1)


def _na_bias_tables(rpb):
    nq, nk = NA_QROWS * GRID_W, NA_BAND * GRID_W
    rows = DEC_SEQ // GRID_W
    nblk = rows // NA_QROWS
    ridx = np.zeros((3, nq, nk), np.int32)
    cidx = np.zeros((3, nq, nk), np.int32)
    valid = np.zeros((3, nq, nk), bool)
    for kind, blk in enumerate((0, 1, nblk - 1)):
        bs = int(np.clip(NA_QROWS * blk - WIN_R // 2, 0, rows - NA_BAND))
        r = NA_QROWS * blk + np.arange(nq) // GRID_W
        c = np.arange(nq) % GRID_W
        ri = bs + np.arange(nk) // GRID_W
        cj = np.arange(nk) % GRID_W
        rs = np.clip(r - WIN_R // 2, 0, rows - WIN_R)
        cs = np.clip(c - WIN_C // 2, 0, GRID_W - WIN_C)
        ok = ((ri[None, :] >= rs[:, None]) & (ri[None, :] < rs[:, None] + WIN_R)
              & (cj[None, :] >= cs[:, None]) & (cj[None, :] < cs[:, None] + WIN_C))
        valid[kind] = ok
        ridx[kind] = np.where(ok, ri[None, :] - r[:, None] + WIN_R - 1, 0)
        cidx[kind] = np.where(ok, cj[None, :] - c[:, None] + WIN_C - 1, 0)
    bias = rpb[:, ridx, cidx]
    bias = jnp.where(valid[None], bias, NEG_BIG)
    return jnp.transpose(bias, (1, 0, 2, 3)).astype(F32)


def _lat_na(z, cache_k, cache_v, bias, layer):
    nq_rows = NA_QROWS * GRID_W
    nblk = DEC_SEQ // nq_rows
    q0 = NCTX // nq_rows
    s0 = NCTX // DEC_SEQ
    kvspec = lambda c: pl.BlockSpec((DEC_SEQ, D_NA), lambda b, i: (s0 + b, c // D_NA))
    cspec = pl.BlockSpec((None, None, PAST_LEN, D_NA), lambda b, i: (b, layer, 0, 0))
    kind = lambda b, i: (jnp.where(i == 0, 0, jnp.where(i == nblk - 1, 2, 1)), 0, 0, 0)
    return pl.pallas_call(
        _lat_na_kernel,
        grid=(DEC_BATCH, nblk),
        in_specs=[pl.BlockSpec((nq_rows, D_NA), lambda b, i: (q0 + b * nblk + i, 0)),
                  kvspec(C_AK), kvspec(C_AV), cspec, cspec,
                  pl.BlockSpec((None, H_NA, nq_rows, NA_BAND * GRID_W), kind)],
        out_specs=pl.BlockSpec((nq_rows, D_NA), lambda b, i: (b * nblk + i, 0)),
        out_shape=jax.ShapeDtypeStruct((NLAT, D_NA), F32),
        compiler_params=_cparams(2),
    )(z, z, z, cache_k, cache_v, bias)


def _ret_kernel(lg_ref, q_ref, k_ref, v_ref, s0_ref, lgl_ref, o_ref, sfin_ref, s_scr, *, n_chunks):
    d = pl.program_id(1)
    c = pl.program_id(2)
    C = RET_CHUNK

    @pl.when(c == 0)
    def _():
        s_scr[...] = s0_ref[...].astype(F32)

    fwd = d == 0
    q = q_ref[...]
    k = k_ref[...]
    vb = v_ref[...].astype(BF16)
    row = lax.broadcasted_iota(jnp.int32, (C, 1), 0).astype(F32)
    pos_q = jnp.where(fwd, row + 1.0, C - row)
    pos_k = jnp.where(fwd, C - 1.0 - row, row)
    lgl = lgl_ref[...]
    q_decay = jnp.exp(lgl * pos_q)
    kd = (k * jnp.exp(lgl * pos_k)).astype(BF16)
    chunk_decay = jnp.exp(lgl * float(C))
    qb = q.astype(BF16)
    kb = k.astype(BF16)
    ii = lax.broadcasted_iota(jnp.int32, (C, C), 0)
    jj = lax.broadcasted_iota(jnp.int32, (C, C), 1)
    delta = jnp.where(fwd, ii - jj, jj - ii).astype(F32)
    outs = []
    for h in range(H_RET):
        lg = lg_ref[d, h]
        mask = jnp.where(delta >= 0.0, jnp.exp(lg * jnp.maximum(delta, 0.0)), 0.0)
        qh, kh, vh = _head(qb, h), _head(kb, h), _head(vb, h)
        inner = _dot((_dot_nt(qh, kh) * mask).astype(BF16), vh)
        s = s_scr[h]
        cross = _dot(qh, s.astype(BF16)) * _head(q_decay, h)
        s_scr[h] = s * _head(chunk_decay, h) + _dot_tn(_head(kd, h), vh)
        outs.append(inner + cross)
    o_ref[...] = jnp.concatenate(outs, axis=-1)

    @pl.when(c == n_chunks - 1)
    def _():
        sfin_ref[...] = s_scr[...]


def _retention(z, s0, log_g, row0, n_batch, seq):
    n = seq // RET_CHUNK
    r0 = row0 // RET_CHUNK
    lgl = jnp.repeat(log_g, HEAD_DIM, axis=-1).reshape(2, 1, D_RET)

    def chunk(b, d, c):
        return r0 + b * n + jnp.where(d == 0, c, n - 1 - c)

    zspec = lambda col: pl.BlockSpec((RET_CHUNK, D_RET), lambda b, d, c, lg: (chunk(b, d, c), col // D_RET))
    sspec = pl.BlockSpec((None, None, H_RET, HEAD_DIM, HEAD_DIM), lambda b, d, c, lg: (b, d, 0, 0, 0))
    return pl.pallas_call(
        functools.partial(_ret_kernel, n_chunks=n),
        grid_spec=pltpu.PrefetchScalarGridSpec(
            num_scalar_prefetch=1,
            grid=(n_batch, 2, n),
            in_specs=[zspec(C_BQ), zspec(C_BK), zspec(C_BV), sspec,
                      pl.BlockSpec((None, 1, D_RET), lambda b, d, c, lg: (d, 0, 0))],
            out_specs=[pl.BlockSpec((None, RET_CHUNK, D_RET),
                                    lambda b, d, c, lg: (d, chunk(b, d, c) - r0, 0)),
                       sspec],
            scratch_shapes=[pltpu.VMEM((H_RET, HEAD_DIM, HEAD_DIM), F32)]),
        out_shape=[jax.ShapeDtypeStruct((2, n_batch * seq, D_RET), F32),
                   jax.ShapeDtypeStruct((n_batch, 2, H_RET, HEAD_DIM, HEAD_DIM), F32)],
        compiler_params=_cparams(3),
    )(log_g, z, z, z, s0, lgl)


def _out_proj_kernel(x_ref, oa_ref, of_ref, ob_ref, bg_ref, oc_ref, w_ref, gate_ref, y_ref):
    ob = of_ref[...] + ob_ref[...]
    bg = bg_ref[...]
    parts = []
    for j in range(D_RET // LANES):
        t = ob[:, j * LANES:(j + 1) * LANES]
        g = bg[:, j * LANES:(j + 1) * LANES]
        parts.append(t * _two_head_rsqrt(t) * (g * jax.nn.sigmoid(g)))
    obn = jnp.concatenate(parts, axis=-1).astype(BF16)
    o = (_dot(oa_ref[...].astype(BF16), w_ref[0:D_NA, :])
         + _dot(obn, w_ref[D_NA:D_NA + D_RET, :])
         + _dot(oc_ref[...].astype(BF16), w_ref[D_NA + D_RET:, :]))
    y_ref[...] = x_ref[...] + gate_ref[...] * o


def _out_proj(x, oa, o_ret, z, oc, w_b, gate, row0, rows):
    tm = ROW_TILE
    r0 = row0 // tm
    loc = lambda i: (i, 0)
    return pl.pallas_call(
        _out_proj_kernel,
        grid=(rows // tm,),
        in_specs=[pl.BlockSpec((tm, D_MODEL), lambda i: (r0 + i, 0)),
                  pl.BlockSpec((tm, D_NA), loc),
                  pl.BlockSpec((None, tm, D_RET), lambda i: (0, i, 0)),
                  pl.BlockSpec((None, tm, D_RET), lambda i: (1, i, 0)),
                  pl.BlockSpec((tm, D_RET), lambda i: (r0 + i, C_BG // D_RET)),
                  pl.BlockSpec((tm, D_GQA), loc),
                  pl.BlockSpec((D_MODEL, D_MODEL), lambda i: (0, 0)),
                  pl.BlockSpec((None, 1, D_MODEL), lambda i: (_mod_index(r0 + i, tm), 0, 0))],
        out_specs=pl.BlockSpec((tm, D_MODEL), lambda i: (r0 + i, 0)),
        out_shape=jax.ShapeDtypeStruct((NTOK, D_MODEL), F32),
        input_output_aliases={0: 0},
        compiler_params=_cparams(1),
    )(x, oa, o_ret, o_ret, z, oc, w_b, gate)


def _router_kernel(x_ref, g_ref, sc_ref, sh_ref, w_ref, b_ref, h_ref, idx_ref, gate_ref):
    h = _rms_mod(x_ref[...], g_ref[...], sc_ref[...], sh_ref[...])
    h_ref[...] = h
    w = w_ref[...]
    w_hi = w.astype(BF16)
    w_lo = (w - w_hi.astype(F32)).astype(BF16)
    h_hi = h.astype(BF16)
    h_lo = (h - h_hi.astype(F32)).astype(BF16)
    logits = _dot(h_hi, w_hi) + (_dot(h_hi, w_lo) + _dot(h_lo, w_hi)) + b_ref[...]
    lane = lax.broadcasted_iota(jnp.int32, logits.shape, 1)
    out_lane = lax.broadcasted_iota(jnp.int32, idx_ref.shape, 1)
    idx_out = jnp.zeros(idx_ref.shape, jnp.int32)
    val_out = jnp.zeros(gate_ref.shape, F32)
    top = None
    denom = None
    for k in range(TOP_K):
        m = jnp.max(logits, axis=-1, keepdims=True)
        i = jnp.min(jnp.where(logits == m, lane, N_EXPERTS), axis=-1, keepdims=True)
        logits = jnp.where(lane == i, -jnp.inf, logits)
        if k == 0:
            top = m
        e = jnp.exp(m - top)
        denom = e if k == 0 else denom + e
        idx_out = jnp.where(out_lane == k, i, idx_out)
        val_out = jnp.where(out_lane == k, e, val_out)
    idx_ref[...] = idx_out
    gate_ref[...] = val_out / denom


def _router(x, g, sc, sh, rw, rb):
    tm = ROW_TILE
    row = lambda i: (i, 0)
    fixed = lambda i: (0, 0)
    mod = lambda i: (_mod_index(i, tm), 0, 0)
    return pl.pallas_call(
        _router_kernel,
        grid=(NTOK // tm,),
        in_specs=[pl.BlockSpec((tm, D_MODEL), row),
                  pl.BlockSpec((1, D_MODEL), fixed),
                  pl.BlockSpec((None, 1, D_MODEL), mod),
                  pl.BlockSpec((None, 1, D_MODEL), mod),
                  pl.BlockSpec((D_MODEL, N_EXPERTS), fixed),
                  pl.BlockSpec((1, N_EXPERTS), fixed)],
        out_specs=[pl.BlockSpec((tm, D_MODEL), row),
                   pl.BlockSpec((tm, LANES), row),
                   pl.BlockSpec((tm, LANES), row)],
        out_shape=[jax.ShapeDtypeStruct((NTOK, D_MODEL), F32),
                   jax.ShapeDtypeStruct((NTOK, LANES), jnp.int32),
                   jax.ShapeDtypeStruct((NTOK, LANES), F32)],
        compiler_params=_cparams(1),
    )(x, g, sc, sh, rw, rb)


def _expert_kernel(be_ref, nv_ref, aid_ref, h_hbm, wg_ref, wl_ref, bg_ref, bl_ref, wd_ref, bd_ref,
                   y_hbm, xbuf, ybuf, gsem, ssem):
    j = pl.program_id(0)

    def gather_copy(r):
        tok = jnp.maximum(aid_ref[0, r], 0) // TOP_K
        return pltpu.make_async_copy(h_hbm.at[pl.ds(tok, 1), :], xbuf.at[pl.ds(r, 1), :], gsem)

    def scatter_copy(r):
        return pltpu.make_async_copy(ybuf.at[pl.ds(r, 1), :], y_hbm.at[pl.ds(aid_ref[0, r], 1), :], ssem)

    @pl.when(j < nv_ref[0])
    def _():
        def start_gather(r, carry):
            gather_copy(r).start()
            return carry

        def wait_gather(r, carry):
            gather_copy(r).wait()
            return carry

        lax.fori_loop(0, MOE_BLOCK, start_gather, 0)
        lax.fori_loop(0, MOE_BLOCK, wait_gather, 0)
        xb = xbuf[...].astype(BF16)
        glu = jnp.minimum(_dot(xb, wg_ref[...]) + bg_ref[...], SWIGLU_LIMIT)
        lin = jnp.clip(_dot(xb, wl_ref[...]) + bl_ref[...], -SWIGLU_LIMIT, SWIGLU_LIMIT)
        act = glu * jax.nn.sigmoid(SWIGLU_ALPHA * glu) * (lin + 1.0)
        ybuf[...] = _dot(act.astype(BF16), wd_ref[...]) + bd_ref[...]

        def start_scatter(r, carry):
            @pl.when(aid_ref[0, r] >= 0)
            def _():
                scatter_copy(r).start()
            return carry

        def wait_scatter(r, carry):
            @pl.when(aid_ref[0, r] >= 0)
            def _():
                scatter_copy(r).wait()
            return carry

        lax.fori_loop(0, MOE_BLOCK, start_scatter, 0)
        lax.fori_loop(0, MOE_BLOCK, wait_scatter, 0)


def _experts(h, block_e, n_valid, slot_aid, wg, wl, bg, bl, wd, bd):
    wmap = lambda j, be, nv: (be[j], 0, 0)
    return pl.pallas_call(
        _expert_kernel,
        grid_spec=pltpu.PrefetchScalarGridSpec(
            num_scalar_prefetch=2,
            grid=(N_MOE_BLOCKS,),
            in_specs=[pl.BlockSpec((None, 1, MOE_BLOCK), lambda j, be, nv: (j, 0, 0),
                                   memory_space=pltpu.SMEM),
                      pl.BlockSpec(memory_space=pl.ANY),
                      pl.BlockSpec((None, D_MODEL, D_FF), wmap),
                      pl.BlockSpec((None, D_MODEL, D_FF), wmap),
                      pl.BlockSpec((None, 1, D_FF), wmap),
                      pl.BlockSpec((None, 1, D_FF), wmap),
                      pl.BlockSpec((None, D_FF, D_MODEL), wmap),
                      pl.BlockSpec((None, 1, D_MODEL), wmap)],
            out_specs=pl.BlockSpec(memory_space=pl.ANY),
            scratch_shapes=[pltpu.VMEM((MOE_BLOCK, D_MODEL), F32),
                            pltpu.VMEM((MOE_BLOCK, D_MODEL), F32),
                            pltpu.SemaphoreType.DMA(()),
                            pltpu.SemaphoreType.DMA(())]),
        out_shape=jax.ShapeDtypeStruct((NK, D_MODEL), F32),
        compiler_params=_cparams(1),
    )(block_e, n_valid, slot_aid, h, wg, wl, bg, bl, wd, bd)


def _dispatch(top_idx):
    flat_e = top_idx.reshape(-1)
    order = jnp.argsort(flat_e, stable=True).astype(jnp.int32)
    sorted_e = flat_e[order]
    counts = jnp.bincount(flat_e, length=N_EXPERTS).astype(jnp.int32)
    padded = ((counts + MOE_BLOCK - 1) // MOE_BLOCK) * MOE_BLOCK
    pad_end = jnp.cumsum(padded)
    pad_start = pad_end - padded
    start = jnp.cumsum(counts) - counts
    dest = pad_start[sorted_e] + jnp.arange(NK, dtype=jnp.int32) - start[sorted_e]
    slot_aid = jnp.full((N_MOE_BLOCKS * MOE_BLOCK,), -1, jnp.int32).at[dest].set(order)
    block_e = jnp.minimum(
        jnp.searchsorted(pad_end, jnp.arange(N_MOE_BLOCKS, dtype=jnp.int32) * MOE_BLOCK, side='right'),
        N_EXPERTS - 1).astype(jnp.int32)
    n_valid = (pad_end[-1] // MOE_BLOCK).astype(jnp.int32).reshape(1)
    last_e = block_e[jnp.maximum(n_valid[0] - 1, 0)]
    block_e = jnp.where(jnp.arange(N_MOE_BLOCKS) < n_valid[0], block_e, last_e)
    return block_e, n_valid, slot_aid.reshape(N_MOE_BLOCKS, 1, MOE_BLOCK)


def _combine_kernel(x_ref, y_ref, gate_ref, mod_ref, o_ref):
    gates = gate_ref[...]
    acc = gates[:, 0:1] * y_ref[:, 0:D_MODEL]
    for k in range(1, TOP_K):
        acc = acc + gates[:, k:k + 1] * y_ref[:, k * D_MODEL:(k + 1) * D_MODEL]
    o_ref[...] = x_ref[...] + mod_ref[...] * acc


def _combine(x, y, gates, gate2):
    tm = 256
    row = lambda i: (i, 0)
    return pl.pallas_call(
        _combine_kernel,
        grid=(NTOK // tm,),
        in_specs=[pl.BlockSpec((tm, D_MODEL), row),
                  pl.BlockSpec((tm, TOP_K * D_MODEL), row),
                  pl.BlockSpec((tm, LANES), row),
                  pl.BlockSpec((None, 1, D_MODEL), lambda i: (_mod_index(i, tm), 0, 0))],
        out_specs=pl.BlockSpec((tm, D_MODEL), row),
        out_shape=jax.ShapeDtypeStruct((NTOK, D_MODEL), F32),
        input_output_aliases={0: 0},
        compiler_params=_cparams(1),
    )(x, y.reshape(NTOK, TOP_K * D_MODEL), gates, gate2)


def _final_norm_kernel(x_ref, g_ref, o_ref):
    x = x_ref[...]
    ms = jnp.mean(x * x, axis=-1, keepdims=True)
    o_ref[...] = x * lax.rsqrt(ms + EPS) * g_ref[...]


def _final_norm(x, g, row0, rows):
    tm = ROW_TILE
    r0 = row0 // tm
    return pl.pallas_call(
        _final_norm_kernel,
        grid=(rows // tm,),
        in_specs=[pl.BlockSpec((tm, D_MODEL), lambda i: (r0 + i, 0)),
                  pl.BlockSpec((1, D_MODEL), lambda i: (0, 0))],
        out_specs=pl.BlockSpec((tm, D_MODEL), lambda i: (i, 0)),
        out_shape=jax.ShapeDtypeStruct((rows, D_MODEL), F32),
        compiler_params=_cparams(1),
    )(x, g)


def kernel(x_prompt, x_sample, cache_na_k, cache_na_v, cache_gqa_k, cache_gqa_v, state_ret, c, c_ctx,
           ada_w, ada_b, norm1_g, norm2_g, w_in, w_out, na_rpb, ret_decay_logit, q_norm_g, k_norm_g,
           router_w, router_b, w_gate_up, b_gate_up, w_down, b_down, final_g):
    x = jnp.concatenate([x_prompt.reshape(NCTX, D_MODEL), x_sample.reshape(NLAT, D_MODEL)], axis=0)
    cond = jnp.zeros((N_COND, D_MODEL), F32).at[0].set(c_ctx).at[1:1 + DEC_BATCH].set(c)
    mod = _modulation(cond, ada_w, ada_b)
    tabs = _rope_tables()
    cache_na_k = cache_na_k.reshape(DEC_BATCH, DEPTH, PAST_LEN, D_NA)
    cache_na_v = cache_na_v.reshape(DEC_BATCH, DEPTH, PAST_LEN, D_NA)
    cache_gqa_k = cache_gqa_k.reshape(DEC_BATCH, DEPTH, PAST_LEN, D_KV)
    cache_gqa_v = cache_gqa_v.reshape(DEC_BATCH, DEPTH, PAST_LEN, D_KV)
    zero_state = jnp.zeros((BATCH, 2, H_RET, HEAD_DIM, HEAD_DIM), F32)
    two_heads = lambda g: jnp.tile(g, LANES // HEAD_DIM).reshape(1, LANES)
    new_na_k, new_na_v, new_gqa_k, new_gqa_v, new_ret = [], [], [], [], []
    for l in range(DEPTH):
        sh1, sc1, g1, sh2, sc2, g2 = [m.reshape(N_COND, 1, D_MODEL) for m in jnp.split(mod[l], 6, axis=-1)]
        z = _in_proj(x, norm1_g[l].reshape(1, D_MODEL), sc1, sh1, w_in[l].astype(BF16), tabs,
                     two_heads(q_norm_g[l]), two_heads(k_norm_g[l]))
        log_g = jax.nn.log_sigmoid(ret_decay_logit[l].astype(F32))
        w_out_b = w_out[l].astype(BF16)
        oa_c, oc_c = _ctx_attention(z)
        oret_c, st = _retention(z, zero_state, log_g, 0, BATCH, SEQ)
        x = _out_proj(x, oa_c, oret_c, z, oc_c, w_out_b, g1, 0, NCTX)
        zc = z[:NCTX]
        new_na_k.append(zc[:, C_AK:C_AV].reshape(BATCH, SEQ, H_NA, HEAD_DIM))
        new_na_v.append(zc[:, C_AV:C_BQ].reshape(BATCH, SEQ, H_NA, HEAD_DIM))
        new_gqa_k.append(zc[:, C_CK:C_CV].reshape(BATCH, SEQ, KV_GQA, HEAD_DIM))
        new_gqa_v.append(zc[:, C_CV:D_IN].reshape(BATCH, SEQ, KV_GQA, HEAD_DIM))
        new_ret.append(st)
        oa_l = _lat_na(z, cache_na_k, cache_na_v, _na_bias_tables(na_rpb[l]), l)
        oc_l = _lat_gqa(z, cache_gqa_k, cache_gqa_v, l)
        oret_l, _ = _retention(z, state_ret[:, l], log_g, NCTX, DEC_BATCH, DEC_SEQ)
        x = _out_proj(x, oa_l, oret_l, z, oc_l, w_out_b, g1, NCTX, NLAT)
        h, top_idx, gates = _router(x, norm2_g[l].reshape(1, D_MODEL), sc2, sh2, router_w[l],
                                    router_b[l].reshape(1, N_EXPERTS))
        block_e, n_valid, slot_aid = _dispatch(top_idx[:, :TOP_K])
        wgu = w_gate_up[l]
        bgu = b_gate_up[l]
        y = _experts(h, block_e, n_valid, slot_aid,
                     wgu[:, :, 0::2].astype(BF16), wgu[:, :, 1::2].astype(BF16),
                     bgu[:, 0::2].reshape(N_EXPERTS, 1, D_FF), bgu[:, 1::2].reshape(N_EXPERTS, 1, D_FF),
                     w_down[l].astype(BF16), b_down[l].reshape(N_EXPERTS, 1, D_MODEL))
        x = _combine(x, y, gates, g2)
    y_prompt = _final_norm(x, final_g.reshape(1, D_MODEL), 0, NCTX).reshape(BATCH, SEQ, D_MODEL)
    y_sample = _final_norm(x, final_g.reshape(1, D_MODEL), NCTX, NLAT).reshape(DEC_BATCH, DEC_SEQ, D_MODEL)
    return (y_prompt, y_sample, jnp.stack(new_na_k, axis=1), jnp.stack(new_na_v, axis=1),
            jnp.stack(new_gqa_k, axis=1), jnp.stack(new_gqa_v, axis=1), jnp.stack(new_ret, axis=1))
```

```python
import functools

import jax
import jax.numpy as jnp
import numpy as np
from jax import lax
from jax.experimental import pallas as pl
from jax.experimental.pallas import tpu as pltpu

F32 = jnp.float32
BF16 = jnp.bfloat16

D_MODEL = 1024
BATCH = 32
SEQ = 256
DEPTH = 2
DEC_BATCH = 4
DEC_SEQ = 4096
PAST_LEN = 256
GRID_W = 64
HEAD_DIM = 64
H_NA = 4
H_RET = 4
H_GQA = 8
KV_GQA = 2
D_NA = H_NA * HEAD_DIM
D_RET = H_RET * HEAD_DIM
D_GQA = H_GQA * HEAD_DIM
D_KV = KV_GQA * HEAD_DIM
D_IN = 3 * D_NA + 4 * D_RET + D_GQA + 2 * D_KV
WIN_R = 8
WIN_C = 16
RET_CHUNK = 128
N_EXPERTS = 32
TOP_K = 4
D_FF = D_MODEL
SWIGLU_LIMIT = 7.0
SWIGLU_ALPHA = 1.702
MOE_BLOCK = 256
ROPE_THETA = 10000.0
EPS = 1e-6
ATTN_SCALE = HEAD_DIM ** -0.5

NCTX = BATCH * SEQ
NLAT = DEC_BATCH * DEC_SEQ
NTOK = NCTX + NLAT
N_COND = 8
NK = NTOK * TOP_K
N_MOE_BLOCKS = NK // MOE_BLOCK + N_EXPERTS
LANES = 128
NEG_BIG = -1e30

C_AQ, C_AK, C_AV = 0, D_NA, 2 * D_NA
C_BQ = 3 * D_NA
C_BK = C_BQ + D_RET
C_BV = C_BK + D_RET
C_BG = C_BV + D_RET
C_CQ = C_BG + D_RET
C_CK = C_CQ + D_GQA
C_CV = C_CK + D_KV

ROW_TILE = 512
NA_QROWS = 4
NA_BAND = 12
GQA_TQ = 128
VMEM_LIMIT = 56 * 1024 * 1024


def _cparams(n_axes):
    return pltpu.CompilerParams(dimension_semantics=("arbitrary",) * n_axes,
                                vmem_limit_bytes=VMEM_LIMIT)


def _mod_index(i, tile):
    nctx = NCTX // tile
    per_batch = DEC_SEQ // tile
    return jnp.where(i < nctx, 0, 1 + (i - nctx) // per_batch)


def _dot(a, b):
    return jnp.dot(a, b, preferred_element_type=F32)


def _dot_nt(a, b):
    return lax.dot_general(a, b, (((1,), (1,)), ((), ())), preferred_element_type=F32)


def _dot_tn(a, b):
    return lax.dot_general(a, b, (((0,), (0,)), ((), ())), preferred_element_type=F32)


def _mod_kernel(c_ref, w_ref, b_ref, o_ref):
    c = c_ref[...]
    s = c * jax.nn.sigmoid(c)
    o_ref[...] = jnp.dot(s, w_ref[...], preferred_element_type=F32,
                         precision=lax.Precision.HIGHEST) + b_ref[...]


def _modulation(cond, ada_w, ada_b):
    tn = 1536
    return pl.pallas_call(
        _mod_kernel, name="modulation",
        grid=(DEPTH, 6 * D_MODEL // tn),
        in_specs=[pl.BlockSpec((N_COND, D_MODEL), lambda l, j: (0, 0)),
                  pl.BlockSpec((None, D_MODEL, tn), lambda l, j: (l, 0, j)),
                  pl.BlockSpec((None, 1, tn), lambda l, j: (l, 0, j))],
        out_specs=pl.BlockSpec((None, N_COND, tn), lambda l, j: (l, 0, j)),
        out_shape=jax.ShapeDtypeStruct((DEPTH, N_COND, 6 * D_MODEL), F32),
        compiler_params=_cparams(2),
    )(cond, ada_w, ada_b.reshape(DEPTH, 1, 6 * D_MODEL))


def _rms_mod(x, g, sc, sh):
    ms = jnp.mean(x * x, axis=-1, keepdims=True)
    return (x * lax.rsqrt(ms + EPS) * g) * (1.0 + sc) + sh


def _two_head_rsqrt(x):
    sq = x * x
    left = lax.broadcasted_iota(jnp.int32, x.shape, 1) < HEAD_DIM
    s_left = jnp.sum(jnp.where(left, sq, 0.0), axis=-1, keepdims=True)
    s_right = jnp.sum(jnp.where(left, 0.0, sq), axis=-1, keepdims=True)
    ms = jnp.where(left, s_left, s_right) * (1.0 / HEAD_DIM)
    return lax.rsqrt(ms + EPS)


def _rope128(x, cos, sin_signed):
    first = (lax.broadcasted_iota(jnp.int32, x.shape, 1) & (HEAD_DIM // 2)) == 0
    rot = jnp.where(first, pltpu.roll(x, LANES - HEAD_DIM // 2, 1), pltpu.roll(x, HEAD_DIM // 2, 1))
    return x * cos + rot * sin_signed


def _softmax_pv(scores, values):
    m = functools.reduce(jnp.maximum, [jnp.max(s, axis=-1, keepdims=True) for s in scores])
    ps = [jnp.exp(s - m) for s in scores]
    denom = functools.reduce(lambda a, b: a + b, [jnp.sum(p, axis=-1, keepdims=True) for p in ps])
    o = functools.reduce(lambda a, b: a + b, [_dot(p.astype(BF16), v) for p, v in zip(ps, values)])
    return o / denom


def _head(x, h):
    return x[:, h * HEAD_DIM:(h + 1) * HEAD_DIM]


def _in_proj_kernel(x_ref, g_ref, sc_ref, sh_ref, w_ref, rc_ref, rs_ref, ac_ref, as_ref,
                    qg_ref, kg_ref, z_ref):
    hb = _rms_mod(x_ref[...], g_ref[...], sc_ref[...], sh_ref[...]).astype(BF16)

    def proj(c0, c1):
        return _dot(hb, w_ref[:, c0:c1])

    z_ref[:, C_AQ:C_AK] = proj(C_AQ, C_AK) * ATTN_SCALE
    z_ref[:, C_AK:C_BQ] = proj(C_AK, C_BQ)
    rc, rs = rc_ref[...], rs_ref[...]
    zq = proj(C_BQ, C_BK)
    zk = proj(C_BK, C_BV)
    for j in range(D_RET // LANES):
        sl = slice(j * LANES, (j + 1) * LANES)
        z_ref[:, C_BQ + j * LANES:C_BQ + (j + 1) * LANES] = _rope128(zq[:, sl], rc, rs)
        z_ref[:, C_BK + j * LANES:C_BK + (j + 1) * LANES] = _rope128(zk[:, sl], rc, rs) * ATTN_SCALE
    z_ref[:, C_BV:C_CQ] = proj(C_BV, C_CQ)
    ac, asn = ac_ref[...], as_ref[...]
    zc = proj(C_CQ, C_CK)
    for j in range(D_GQA // LANES):
        t = zc[:, j * LANES:(j + 1) * LANES]
        t = t * _two_head_rsqrt(t) * qg_ref[...]
        z_ref[:, C_CQ + j * LANES:C_CQ + (j + 1) * LANES] = _rope128(t, ac, asn) * ATTN_SCALE
    zkv = proj(C_CK, D_IN)
    t = zkv[:, :LANES]
    t = t * _two_head_rsqrt(t) * kg_ref[...]
    z_ref[:, C_CK:C_CV] = _rope128(t, ac, asn)
    z_ref[:, C_CV:D_IN] = zkv[:, LANES:]


def _in_proj(x, g, sc, sh, w_b, tabs, qg, kg):
    tm = ROW_TILE
    nctx = NCTX // tm
    per_seq = DEC_SEQ // tm

    def tab_idx(i):
        return (jnp.where(i < nctx, per_seq, (i - nctx) % per_seq), 0)

    row = lambda i: (i, 0)
    fixed = lambda i: (0, 0)
    mod = lambda i: (_mod_index(i, tm), 0, 0)
    tab_spec = pl.BlockSpec((tm, LANES), tab_idx)
    return pl.pallas_call(
        _in_proj_kernel, name="in_proj",
        grid=(NTOK // tm,),
        in_specs=[pl.BlockSpec((tm, D_MODEL), row),
                  pl.BlockSpec((1, D_MODEL), fixed),
                  pl.BlockSpec((None, 1, D_MODEL), mod),
                  pl.BlockSpec((None, 1, D_MODEL), mod),
                  pl.BlockSpec((D_MODEL, D_IN), fixed),
                  tab_spec, tab_spec, tab_spec, tab_spec,
                  pl.BlockSpec((1, LANES), fixed),
                  pl.BlockSpec((1, LANES), fixed)],
        out_specs=pl.BlockSpec((tm, D_IN), row),
        out_shape=jax.ShapeDtypeStruct((NTOK, D_IN), F32),
        compiler_params=_cparams(1),
    )(x, g, sc, sh, w_b, *tabs, qg, kg)


def _rope_tables():
    t = np.arange(DEC_SEQ)
    inv_ret = 1.0 / (ROPE_THETA ** np.linspace(0.0, 1.0, HEAD_DIM // 2, dtype=np.float32))
    ang_ret = t.astype(np.float32)[:, None] * inv_ret.astype(np.float32)
    n = HEAD_DIM // 4
    inv_ax = (ROPE_THETA ** (-np.arange(n, dtype=np.float32) / n)).astype(np.float32)
    row = (t // GRID_W).astype(np.float32)
    col = (t % GRID_W).astype(np.float32)
    ang_ax = np.concatenate([row[:, None] * inv_ax, col[:, None] * inv_ax], axis=-1)

    def tables(ang):
        ang = jnp.asarray(ang, F32)
        cos, sin = jnp.cos(ang), jnp.sin(ang)
        cos2 = jnp.tile(jnp.concatenate([cos, cos], axis=-1), (1, LANES // HEAD_DIM))
        sin2 = jnp.tile(jnp.concatenate([-sin, sin], axis=-1), (1, LANES // HEAD_DIM))
        cos2 = jnp.concatenate([cos2, jnp.ones((ROW_TILE, LANES), F32)], axis=0)
        sin2 = jnp.concatenate([sin2, jnp.zeros((ROW_TILE, LANES), F32)], axis=0)
        return cos2, sin2

    return (*tables(ang_ret), *tables(ang_ax))


def _ctx_attn_kernel(aq_ref, ak_ref, av_ref, cq0_ref, cq1_ref, ck_ref, cv_ref, oa_ref, oc_ref):
    aq = aq_ref[...].astype(BF16)
    ak = ak_ref[...].astype(BF16)
    av = av_ref[...].astype(BF16)
    outs = []
    for h in range(H_NA):
        s = _dot_nt(_head(aq, h), _head(ak, h))
        outs.append(_softmax_pv([s], [_head(av, h)]))
    oa_ref[...] = jnp.concatenate(outs, axis=-1)
    ck = ck_ref[...].astype(BF16)
    cv = cv_ref[...].astype(BF16)
    group = H_GQA // KV_GQA
    outs = []
    for g, cq_ref in enumerate((cq0_ref, cq1_ref)):
        cq = cq_ref[...].astype(BF16)
        qs = jnp.concatenate([_head(cq, j) for j in range(group)], axis=0)
        o = _softmax_pv([_dot_nt(qs, _head(ck, g))], [_head(cv, g)])
        outs += [o[j * SEQ:(j + 1) * SEQ] for j in range(group)]
    oc_ref[...] = jnp.concatenate(outs, axis=-1)


def _ctx_attention(z):
    col = lambda c, w: pl.BlockSpec((SEQ, w), lambda b: (b, c // w))
    return pl.pallas_call(
        _ctx_attn_kernel, name="ctx_attn",
        grid=(BATCH,),
        in_specs=[col(C_AQ, D_NA), col(C_AK, D_NA), col(C_AV, D_NA),
                  col(C_CQ, D_NA), col(C_CQ + D_NA, D_NA), col(C_CK, D_KV), col(C_CV, D_KV)],
        out_specs=[pl.BlockSpec((SEQ, D_NA), lambda b: (b, 0)),
                   pl.BlockSpec((SEQ, D_GQA), lambda b: (b, 0))],
        out_shape=[jax.ShapeDtypeStruct((NCTX, D_NA), F32),
                   jax.ShapeDtypeStruct((NCTX, D_GQA), F32)],
        compiler_params=_cparams(1),
    )(z, z, z, z, z, z, z)


def _lat_gqa_kernel(cq0_ref, cq1_ref, ck_ref, cv_ref, kc_ref, vc_ref, oc_ref):
    ck = ck_ref[...].astype(BF16)
    cv = cv_ref[...].astype(BF16)
    kc = kc_ref[...].astype(BF16)
    vc = vc_ref[...].astype(BF16)
    group = H_GQA // KV_GQA
    outs = []
    for g, cq_ref in enumerate((cq0_ref, cq1_ref)):
        cq = cq_ref[...].astype(BF16)
        qs = jnp.concatenate([_head(cq, j) for j in range(group)], axis=0)
        o = _softmax_pv([_dot_nt(qs, _head(kc, g)), _dot_nt(qs, _head(ck, g))],
                        [_head(vc, g), _head(cv, g)])
        outs += [o[j * GQA_TQ:(j + 1) * GQA_TQ] for j in range(group)]
    oc_ref[...] = jnp.concatenate(outs, axis=-1)


def _lat_gqa(z, cache_k, cache_v, layer):
    nq = DEC_SEQ // GQA_TQ
    q0 = NCTX // GQA_TQ
    s0 = NCTX // DEC_SEQ
    qspec = lambda c: pl.BlockSpec((GQA_TQ, D_NA), lambda b, i: (q0 + b * nq + i, c // D_NA))
    kvspec = lambda c: pl.BlockSpec((DEC_SEQ, D_KV), lambda b, i: (s0 + b, c // D_KV))
    cspec = pl.BlockSpec((None, None, PAST_LEN, D_KV), lambda b, i: (b, layer, 0, 0))
    return pl.pallas_call(
        _lat_gqa_kernel, name="lat_gqa",
        grid=(DEC_BATCH, nq),
        in_specs=[qspec(C_CQ), qspec(C_CQ + D_NA), kvspec(C_CK), kvspec(C_CV), cspec, cspec],
        out_specs=pl.BlockSpec((GQA_TQ, D_GQA), lambda b, i: (b * nq + i, 0)),
        out_shape=jax.ShapeDtypeStruct((NLAT, D_GQA), F32),
        compiler_params=_cparams(2),
    )(z, z, z, z, cache_k, cache_v)


def _na_band_start(i):
    return jnp.clip(NA_QROWS * i - WIN_R // 2, 0, GRID_W - NA_BAND)


def _lat_na_kernel(q_ref, k_ref, v_ref, kc_ref, vc_ref, bias_ref, o_ref):
    i = pl.program_id(1)
    start = pl.multiple_of(_na_band_start(i) * GRID_W, GRID_W)
    q = q_ref[...].astype(BF16)
    kb = k_ref[pl.ds(start, NA_BAND * GRID_W), :].astype(BF16)
    vb = v_ref[pl.ds(start, NA_BAND * GRID_W), :].astype(BF16)
    kc = kc_ref[...].astype(BF16)
    vc = vc_ref[...].astype(BF16)
    outs = []
    for h in range(H_NA):
        qh = _head(q, h)
        s_loc = _dot_nt(qh, _head(kb, h)) + bias_ref[h]
        s_ctx = _dot_nt(qh, _head(kc, h))
        outs.append(_softmax_pv([s_loc, s_ctx], [_head(vb, h), _head(vc, h)]))
    o_ref[...] = jnp.concatenate(outs, axis=-1)


def _na_bias_tables(rpb):
    nq, nk = NA_QROWS * GRID_W, NA_BAND * GRID_W
    rows = DEC_SEQ // GRID_W
    nblk = rows // NA_QROWS
    n_rb, n_cb = 2 * WIN_R - 1, 2 * WIN_C - 1
    row_sel = np.zeros((3, NA_QROWS, NA_BAND, n_rb), np.float32)
    row_ok = np.zeros((3, NA_QROWS, NA_BAND), bool)
    for kind, blk in enumerate((0, 1, nblk - 1)):
        bs = int(np.clip(NA_QROWS * blk - WIN_R // 2, 0, rows - NA_BAND))
        for jr in range(NA_QROWS):
            r = NA_QROWS * blk + jr
            rs = int(np.clip(r - WIN_R // 2, 0, rows - WIN_R))
            for bi in range(NA_BAND):
                ri = bs + bi
                if rs <= ri < rs + WIN_R:
                    row_ok[kind, jr, bi] = True
                    row_sel[kind, jr, bi, ri - r + WIN_R - 1] = 1.0
    c = np.arange(GRID_W)
    cs = np.clip(c - WIN_C // 2, 0, GRID_W - WIN_C)
    col_ok = (c[None, :] >= cs[:, None]) & (c[None, :] < cs[:, None] + WIN_C)
    col_sel = np.zeros((n_cb, GRID_W, GRID_W), np.float32)
    qc, kc = np.nonzero(col_ok)
    col_sel[kc - qc + WIN_C - 1, qc, kc] = 1.0
    hi = lax.Precision.HIGHEST
    by_row = jnp.einsum('kjba,hax->hkjbx', row_sel, rpb, precision=hi)
    bias = jnp.einsum('hkjbx,xcz->khjcbz', by_row, col_sel, precision=hi)
    ok = row_ok[:, None, :, None, :, None] & col_ok[None, None, None, :, None, :]
    return jnp.where(ok, bias, NEG_BIG).reshape(3, H_NA, nq, nk).astype(F32)


def _lat_na(z, cache_k, cache_v, bias, layer):
    nq_rows = NA_QROWS * GRID_W
    nblk = DEC_SEQ // nq_rows
    q0 = NCTX // nq_rows
    s0 = NCTX // DEC_SEQ
    kvspec = lambda c: pl.BlockSpec((DEC_SEQ, D_NA), lambda b, i: (s0 + b, c // D_NA))
    cspec = pl.BlockSpec((None, None, PAST_LEN, D_NA), lambda b, i: (b, layer, 0, 0))
    kind = lambda b, i: (jnp.where(i == 0, 0, jnp.where(i == nblk - 1, 2, 1)), 0, 0, 0)
    return pl.pallas_call(
        _lat_na_kernel, name="lat_na",
        grid=(DEC_BATCH, nblk),
        in_specs=[pl.BlockSpec((nq_rows, D_NA), lambda b, i: (q0 + b * nblk + i, 0)),
                  kvspec(C_AK), kvspec(C_AV), cspec, cspec,
                  pl.BlockSpec((None, H_NA, nq_rows, NA_BAND * GRID_W), kind)],
        out_specs=pl.BlockSpec((nq_rows, D_NA), lambda b, i: (b * nblk + i, 0)),
        out_shape=jax.ShapeDtypeStruct((NLAT, D_NA), F32),
        compiler_params=_cparams(2),
    )(z, z, z, cache_k, cache_v, bias)


def _ret_kernel(lg_ref, q_ref, k_ref, v_ref, s0_ref, lgl_ref, o_ref, sfin_ref, s_scr, *, n_chunks):
    d = pl.program_id(1)
    c = pl.program_id(2)
    C = RET_CHUNK

    @pl.when(c == 0)
    def _():
        s_scr[...] = s0_ref[...].astype(F32)

    fwd = d == 0
    q = q_ref[...]
    k = k_ref[...]
    vb = v_ref[...].astype(BF16)
    row = lax.broadcasted_iota(jnp.int32, (C, 1), 0).astype(F32)
    pos_q = jnp.where(fwd, row + 1.0, C - row)
    pos_k = jnp.where(fwd, C - 1.0 - row, row)
    lgl = lgl_ref[...]
    q_decay = jnp.exp(lgl * pos_q)
    kd = (k * jnp.exp(lgl * pos_k)).astype(BF16)
    chunk_decay = jnp.exp(lgl * float(C))
    qb = q.astype(BF16)
    kb = k.astype(BF16)
    ii = lax.broadcasted_iota(jnp.int32, (C, C), 0)
    jj = lax.broadcasted_iota(jnp.int32, (C, C), 1)
    delta = jnp.where(fwd, ii - jj, jj - ii).astype(F32)
    outs = []
    for h in range(H_RET):
        lg = lg_ref[d, h]
        mask = jnp.where(delta >= 0.0, jnp.exp(lg * jnp.maximum(delta, 0.0)), 0.0)
        qh, kh, vh = _head(qb, h), _head(kb, h), _head(vb, h)
        inner = _dot((_dot_nt(qh, kh) * mask).astype(BF16), vh)
        s = s_scr[h]
        cross = _dot(qh, s.astype(BF16)) * _head(q_decay, h)
        s_scr[h] = s * _head(chunk_decay, h) + _dot_tn(_head(kd, h), vh)
        outs.append(inner + cross)
    o_ref[...] = jnp.concatenate(outs, axis=-1)

    @pl.when(c == n_chunks - 1)
    def _():
        sfin_ref[...] = s_scr[...]


def _retention(z, s0, log_g, row0, n_batch, seq):
    n = seq // RET_CHUNK
    r0 = row0 // RET_CHUNK
    lgl = jnp.repeat(log_g, HEAD_DIM, axis=-1).reshape(2, 1, D_RET)

    def chunk(b, d, c):
        return r0 + b * n + jnp.where(d == 0, c, n - 1 - c)

    zspec = lambda col: pl.BlockSpec((RET_CHUNK, D_RET), lambda b, d, c, lg: (chunk(b, d, c), col // D_RET))
    sspec = pl.BlockSpec((None, None, H_RET, HEAD_DIM, HEAD_DIM), lambda b, d, c, lg: (b, d, 0, 0, 0))
    return pl.pallas_call(
        functools.partial(_ret_kernel, n_chunks=n), name="retention",
        grid_spec=pltpu.PrefetchScalarGridSpec(
            num_scalar_prefetch=1,
            grid=(n_batch, 2, n),
            in_specs=[zspec(C_BQ), zspec(C_BK), zspec(C_BV), sspec,
                      pl.BlockSpec((None, 1, D_RET), lambda b, d, c, lg: (d, 0, 0))],
            out_specs=[pl.BlockSpec((None, RET_CHUNK, D_RET),
                                    lambda b, d, c, lg: (d, chunk(b, d, c) - r0, 0)),
                       sspec],
            scratch_shapes=[pltpu.VMEM((H_RET, HEAD_DIM, HEAD_DIM), F32)]),
        out_shape=[jax.ShapeDtypeStruct((2, n_batch * seq, D_RET), F32),
                   jax.ShapeDtypeStruct((n_batch, 2, H_RET, HEAD_DIM, HEAD_DIM), F32)],
        compiler_params=_cparams(3),
    )(log_g, z, z, z, s0, lgl)


def _out_proj_kernel(x_ref, oa_ref, of_ref, ob_ref, bg_ref, oc_ref, w_ref, gate_ref, y_ref):
    ob = of_ref[...] + ob_ref[...]
    bg = bg_ref[...]
    parts = []
    for j in range(D_RET // LANES):
        t = ob[:, j * LANES:(j + 1) * LANES]
        g = bg[:, j * LANES:(j + 1) * LANES]
        parts.append(t * _two_head_rsqrt(t) * (g * jax.nn.sigmoid(g)))
    obn = jnp.concatenate(parts, axis=-1).astype(BF16)
    o = (_dot(oa_ref[...].astype(BF16), w_ref[0:D_NA, :])
         + _dot(obn, w_ref[D_NA:D_NA + D_RET, :])
         + _dot(oc_ref[...].astype(BF16), w_ref[D_NA + D_RET:, :]))
    y_ref[...] = x_ref[...] + gate_ref[...] * o


def _out_proj(x, oa, o_ret, z, oc, w_b, gate, row0, rows):
    tm = ROW_TILE
    r0 = row0 // tm
    loc = lambda i: (i, 0)
    return pl.pallas_call(
        _out_proj_kernel, name="out_proj",
        grid=(rows // tm,),
        in_specs=[pl.BlockSpec((tm, D_MODEL), lambda i: (r0 + i, 0)),
                  pl.BlockSpec((tm, D_NA), loc),
                  pl.BlockSpec((None, tm, D_RET), lambda i: (0, i, 0)),
                  pl.BlockSpec((None, tm, D_RET), lambda i: (1, i, 0)),
                  pl.BlockSpec((tm, D_RET), lambda i: (r0 + i, C_BG // D_RET)),
                  pl.BlockSpec((tm, D_GQA), loc),
                  pl.BlockSpec((D_MODEL, D_MODEL), lambda i: (0, 0)),
                  pl.BlockSpec((None, 1, D_MODEL), lambda i: (_mod_index(r0 + i, tm), 0, 0))],
        out_specs=pl.BlockSpec((tm, D_MODEL), lambda i: (r0 + i, 0)),
        out_shape=jax.ShapeDtypeStruct((NTOK, D_MODEL), F32),
        input_output_aliases={0: 0},
        compiler_params=_cparams(1),
    )(x, oa, o_ret, o_ret, z, oc, w_b, gate)


def _router_kernel(x_ref, g_ref, sc_ref, sh_ref, w_ref, b_ref, h_ref, idx_ref, gate_ref):
    h = _rms_mod(x_ref[...], g_ref[...], sc_ref[...], sh_ref[...])
    h_ref[...] = h
    w = w_ref[...]
    w_hi = w.astype(BF16)
    w_lo = (w - w_hi.astype(F32)).astype(BF16)
    h_hi = h.astype(BF16)
    h_lo = (h - h_hi.astype(F32)).astype(BF16)
    logits = _dot(h_hi, w_hi) + (_dot(h_hi, w_lo) + _dot(h_lo, w_hi)) + b_ref[...]
    lane = lax.broadcasted_iota(jnp.int32, logits.shape, 1)
    out_lane = lax.broadcasted_iota(jnp.int32, idx_ref.shape, 1)
    idx_out = jnp.zeros(idx_ref.shape, jnp.int32)
    val_out = jnp.zeros(gate_ref.shape, F32)
    top = None
    denom = None
    for k in range(TOP_K):
        m = jnp.max(logits, axis=-1, keepdims=True)
        i = jnp.min(jnp.where(logits == m, lane, N_EXPERTS), axis=-1, keepdims=True)
        logits = jnp.where(lane == i, -jnp.inf, logits)
        if k == 0:
            top = m
        e = jnp.exp(m - top)
        denom = e if k == 0 else denom + e
        idx_out = jnp.where(out_lane == k, i, idx_out)
        val_out = jnp.where(out_lane == k, e, val_out)
    idx_ref[...] = idx_out
    gate_ref[...] = val_out / denom


def _router(x, g, sc, sh, rw, rb):
    tm = ROW_TILE
    row = lambda i: (i, 0)
    fixed = lambda i: (0, 0)
    mod = lambda i: (_mod_index(i, tm), 0, 0)
    return pl.pallas_call(
        _router_kernel, name="router",
        grid=(NTOK // tm,),
        in_specs=[pl.BlockSpec((tm, D_MODEL), row),
                  pl.BlockSpec((1, D_MODEL), fixed),
                  pl.BlockSpec((None, 1, D_MODEL), mod),
                  pl.BlockSpec((None, 1, D_MODEL), mod),
                  pl.BlockSpec((D_MODEL, N_EXPERTS), fixed),
                  pl.BlockSpec((1, N_EXPERTS), fixed)],
        out_specs=[pl.BlockSpec((tm, D_MODEL), row),
                   pl.BlockSpec((tm, LANES), row),
                   pl.BlockSpec((tm, LANES), row)],
        out_shape=[jax.ShapeDtypeStruct((NTOK, D_MODEL), F32),
                   jax.ShapeDtypeStruct((NTOK, LANES), jnp.int32),
                   jax.ShapeDtypeStruct((NTOK, LANES), F32)],
        compiler_params=_cparams(1),
    )(x, g, sc, sh, rw, rb)


PAD_ROW0 = TOP_K * NTOK
Y_ROWS = PAD_ROW0 + 2 * MOE_BLOCK
GU_CHUNK = 2 * LANES


def _expert_kernel(be_ref, nv_ref, src0_ref, srcn_ref, dst_ref, h_hbm, wgu_ref, bg_ref, bl_ref, wd_ref,
                   bd_ref, y_hbm, xbuf, ybuf, wg_s, wl_s, wd_s, t_scr, gsem, ssem):
    j = pl.program_id(0)
    nv = nv_ref[0]
    cur = j % 2
    nxt = 1 - cur

    def issue_gather(src_ref, slot):
        for r in range(MOE_BLOCK):
            pltpu.make_async_copy(h_hbm.at[pl.ds(src_ref[0, r], 1), :],
                                  xbuf.at[slot, pl.ds(r, 1), :], gsem.at[slot]).start()

    def wait_gather(slot):
        pltpu.make_async_copy(xbuf.at[slot], xbuf.at[slot], gsem.at[slot]).wait()

    def wait_scatter(slot):
        pltpu.make_async_copy(ybuf.at[slot], ybuf.at[slot], ssem.at[slot]).wait()

    @pl.when(j == 0)
    def _():
        issue_gather(src0_ref, 0)
        ybuf[...] = jnp.zeros(ybuf.shape, F32)
        for slot in range(2):
            for r in range(MOE_BLOCK):
                pltpu.make_async_copy(ybuf.at[slot, pl.ds(r, 1), :],
                                      y_hbm.at[pl.ds(PAD_ROW0 + slot * MOE_BLOCK + r, 1), :],
                                      ssem.at[slot]).start()

    changed = jnp.logical_or(j == 0, be_ref[j] != be_ref[jnp.maximum(j - 1, 0)])

    @pl.when(jnp.logical_and(j < nv, changed))
    def _():
        for c in range(2 * D_FF // GU_CHUNK):
            t = wgu_ref[:, c * GU_CHUNK:(c + 1) * GU_CHUNK].T
            for kk in range(D_MODEL // LANES):
                t_scr[kk] = t[:, kk * LANES:(kk + 1) * LANES]
            rows = slice(c * GU_CHUNK // 2, (c + 1) * GU_CHUNK // 2)
            for kk in range(D_MODEL // LANES):
                cols = slice(kk * LANES, (kk + 1) * LANES)
                wg_s[rows, cols] = t_scr[kk, pl.ds(0, GU_CHUNK // 2, stride=2), :].astype(BF16)
                wl_s[rows, cols] = t_scr[kk, pl.ds(1, GU_CHUNK // 2, stride=2), :].astype(BF16)
        wd_s[...] = wd_ref[...].astype(BF16)

    @pl.when(j < nv)
    def _():
        wait_scatter(cur)
        wait_gather(cur)
        issue_gather(srcn_ref, nxt)
        xb = xbuf[cur].astype(BF16)
        glu = jnp.minimum(_dot_nt(xb, wg_s[...]) + bg_ref[...], SWIGLU_LIMIT)
        lin = jnp.clip(_dot_nt(xb, wl_s[...]) + bl_ref[...], -SWIGLU_LIMIT, SWIGLU_LIMIT)
        act = glu * jax.nn.sigmoid(SWIGLU_ALPHA * glu) * (lin + 1.0)
        ybuf[cur] = _dot(act.astype(BF16), wd_s[...]) + bd_ref[...]
        for r in range(MOE_BLOCK):
            pltpu.make_async_copy(ybuf.at[cur, pl.ds(r, 1), :],
                                  y_hbm.at[pl.ds(dst_ref[0, r], 1), :], ssem.at[cur]).start()

    @pl.when(j == nv - 1)
    def _():
        wait_gather(nxt)
        wait_scatter(nxt)
        wait_scatter(cur)


def _experts(h, block_e, n_valid, slot_src, slot_dst, wgu, bg, bl, wd, bd, layer):
    wmap = lambda j, be, nv: (be[j], 0, 0)
    wmap_l = lambda j, be, nv: (layer, be[j], 0, 0)
    idx_spec = lambda fn: pl.BlockSpec((None, 1, MOE_BLOCK), fn, memory_space=pltpu.SMEM)
    return pl.pallas_call(
        _expert_kernel,
        grid_spec=pltpu.PrefetchScalarGridSpec(
            num_scalar_prefetch=2,
            grid=(N_MOE_BLOCKS,),
            in_specs=[idx_spec(lambda j, be, nv: (0, 0, 0)),
                      idx_spec(lambda j, be, nv: (jnp.minimum(j + 1, N_MOE_BLOCKS - 1), 0, 0)),
                      idx_spec(lambda j, be, nv: (j, 0, 0)),
                      pl.BlockSpec(memory_space=pl.ANY),
                      pl.BlockSpec((None, None, D_MODEL, 2 * D_FF), wmap_l),
                      pl.BlockSpec((None, 1, D_FF), wmap),
                      pl.BlockSpec((None, 1, D_FF), wmap),
                      pl.BlockSpec((None, None, D_FF, D_MODEL), wmap_l),
                      pl.BlockSpec((None, 1, D_MODEL), wmap)],
            out_specs=pl.BlockSpec(memory_space=pl.ANY),
            scratch_shapes=[pltpu.VMEM((2, MOE_BLOCK, D_MODEL), F32),
                            pltpu.VMEM((2, MOE_BLOCK, D_MODEL), F32),
                            pltpu.VMEM((D_FF, D_MODEL), BF16),
                            pltpu.VMEM((D_FF, D_MODEL), BF16),
                            pltpu.VMEM((D_FF, D_MODEL), BF16),
                            pltpu.VMEM((D_MODEL // LANES, GU_CHUNK, LANES), F32),
                            pltpu.SemaphoreType.DMA((2,)),
                            pltpu.SemaphoreType.DMA((2,))]),
        out_shape=jax.ShapeDtypeStruct((Y_ROWS, D_MODEL), F32),
        compiler_params=_cparams(1),
        name="experts",
    )(block_e, n_valid, slot_src, slot_src, slot_dst, h, wgu, bg, bl, wd, bd)


def _dispatch(top_idx):
    n_slots = N_MOE_BLOCKS * MOE_BLOCK
    flat_e = top_idx.reshape(-1)
    order = jnp.argsort(flat_e, stable=True).astype(jnp.int32)
    sorted_e = flat_e[order]
    counts = jnp.bincount(flat_e, length=N_EXPERTS).astype(jnp.int32)
    padded = ((counts + MOE_BLOCK - 1) // MOE_BLOCK) * MOE_BLOCK
    pad_end = jnp.cumsum(padded)
    pad_start = pad_end - padded
    start = jnp.cumsum(counts) - counts
    dest = pad_start[sorted_e] + jnp.arange(NK, dtype=jnp.int32) - start[sorted_e]
    slot_aid = jnp.full((n_slots,), -1, jnp.int32).at[dest].set(order)
    slot = jnp.arange(n_slots, dtype=jnp.int32)
    pad_dst = PAD_ROW0 + ((slot // MOE_BLOCK) % 2) * MOE_BLOCK + slot % MOE_BLOCK
    slot_src = jnp.where(slot_aid >= 0, slot_aid // TOP_K, 0)
    slot_dst = jnp.where(slot_aid >= 0, (slot_aid % TOP_K) * NTOK + slot_aid // TOP_K, pad_dst)
    block_e = jnp.minimum(
        jnp.searchsorted(pad_end, jnp.arange(N_MOE_BLOCKS, dtype=jnp.int32) * MOE_BLOCK, side='right'),
        N_EXPERTS - 1).astype(jnp.int32)
    n_valid = (pad_end[-1] // MOE_BLOCK).astype(jnp.int32).reshape(1)
    last_e = block_e[jnp.maximum(n_valid[0] - 1, 0)]
    block_e = jnp.where(jnp.arange(N_MOE_BLOCKS) < n_valid[0], block_e, last_e)
    shape = (N_MOE_BLOCKS, 1, MOE_BLOCK)
    return block_e, n_valid, slot_src.reshape(shape), slot_dst.reshape(shape)


def _combine_kernel(x_ref, *refs):
    y_refs, (gate_ref, mod_ref, o_ref) = refs[:TOP_K], refs[TOP_K:]
    gates = gate_ref[...]
    acc = gates[:, 0:1] * y_refs[0][...]
    for k in range(1, TOP_K):
        acc = acc + gates[:, k:k + 1] * y_refs[k][...]
    o_ref[...] = x_ref[...] + mod_ref[...] * acc


def _combine(x, y, gates, gate2):
    tm = ROW_TILE
    row = lambda i: (i, 0)
    plane = lambda k: pl.BlockSpec((tm, D_MODEL), lambda i: (k * (NTOK // tm) + i, 0))
    return pl.pallas_call(
        _combine_kernel,
        grid=(NTOK // tm,),
        in_specs=[pl.BlockSpec((tm, D_MODEL), row)] + [plane(k) for k in range(TOP_K)]
        + [pl.BlockSpec((tm, LANES), row),
           pl.BlockSpec((None, 1, D_MODEL), lambda i: (_mod_index(i, tm), 0, 0))],
        out_specs=pl.BlockSpec((tm, D_MODEL), row),
        out_shape=jax.ShapeDtypeStruct((NTOK, D_MODEL), F32),
        input_output_aliases={0: 0},
        compiler_params=_cparams(1),
        name="combine",
    )(x, *([y] * TOP_K), gates, gate2)


def _final_norm_kernel(x_ref, g_ref, o_ref):
    x = x_ref[...]
    ms = jnp.mean(x * x, axis=-1, keepdims=True)
    o_ref[...] = x * lax.rsqrt(ms + EPS) * g_ref[...]


def _final_norm(x, g, row0, rows):
    tm = ROW_TILE
    r0 = row0 // tm
    return pl.pallas_call(
        _final_norm_kernel, name="final_norm",
        grid=(rows // tm,),
        in_specs=[pl.BlockSpec((tm, D_MODEL), lambda i: (r0 + i, 0)),
                  pl.BlockSpec((1, D_MODEL), lambda i: (0, 0))],
        out_specs=pl.BlockSpec((tm, D_MODEL), lambda i: (i, 0)),
        out_shape=jax.ShapeDtypeStruct((rows, D_MODEL), F32),
        compiler_params=_cparams(1),
    )(x, g)


def kernel(x_prompt, x_sample, cache_na_k, cache_na_v, cache_gqa_k, cache_gqa_v, state_ret, c, c_ctx,
           ada_w, ada_b, norm1_g, norm2_g, w_in, w_out, na_rpb, ret_decay_logit, q_norm_g, k_norm_g,
           router_w, router_b, w_gate_up, b_gate_up, w_down, b_down, final_g):
    x = jnp.concatenate([x_prompt.reshape(NCTX, D_MODEL), x_sample.reshape(NLAT, D_MODEL)], axis=0)
    cond = jnp.zeros((N_COND, D_MODEL), F32).at[0].set(c_ctx).at[1:1 + DEC_BATCH].set(c)
    mod = _modulation(cond, ada_w, ada_b)
    tabs = _rope_tables()
    cache_na_k = cache_na_k.reshape(DEC_BATCH, DEPTH, PAST_LEN, D_NA)
    cache_na_v = cache_na_v.reshape(DEC_BATCH, DEPTH, PAST_LEN, D_NA)
    cache_gqa_k = cache_gqa_k.reshape(DEC_BATCH, DEPTH, PAST_LEN, D_KV)
    cache_gqa_v = cache_gqa_v.reshape(DEC_BATCH, DEPTH, PAST_LEN, D_KV)
    zero_state = jnp.zeros((BATCH, 2, H_RET, HEAD_DIM, HEAD_DIM), F32)
    two_heads = lambda g: jnp.tile(g, LANES // HEAD_DIM).reshape(1, LANES)
    new_na_k, new_na_v, new_gqa_k, new_gqa_v, new_ret = [], [], [], [], []
    for l in range(DEPTH):
        sh1, sc1, g1, sh2, sc2, g2 = [m.reshape(N_COND, 1, D_MODEL) for m in jnp.split(mod[l], 6, axis=-1)]
        z = _in_proj(x, norm1_g[l].reshape(1, D_MODEL), sc1, sh1, w_in[l].astype(BF16), tabs,
                     two_heads(q_norm_g[l]), two_heads(k_norm_g[l]))
        log_g = jax.nn.log_sigmoid(ret_decay_logit[l].astype(F32))
        w_out_b = w_out[l].astype(BF16)
        oa_c, oc_c = _ctx_attention(z)
        oret_c, st = _retention(z, zero_state, log_g, 0, BATCH, SEQ)
        x = _out_proj(x, oa_c, oret_c, z, oc_c, w_out_b, g1, 0, NCTX)
        zc = z[:NCTX]
        new_na_k.append(zc[:, C_AK:C_AV].reshape(BATCH, SEQ, H_NA, HEAD_DIM))
        new_na_v.append(zc[:, C_AV:C_BQ].reshape(BATCH, SEQ, H_NA, HEAD_DIM))
        new_gqa_k.append(zc[:, C_CK:C_CV].reshape(BATCH, SEQ, KV_GQA, HEAD_DIM))
        new_gqa_v.append(zc[:, C_CV:D_IN].reshape(BATCH, SEQ, KV_GQA, HEAD_DIM))
        new_ret.append(st)
        oa_l = _lat_na(z, cache_na_k, cache_na_v, _na_bias_tables(na_rpb[l]), l)
        oc_l = _lat_gqa(z, cache_gqa_k, cache_gqa_v, l)
        oret_l, _ = _retention(z, state_ret[:, l], log_g, NCTX, DEC_BATCH, DEC_SEQ)
        x = _out_proj(x, oa_l, oret_l, z, oc_l, w_out_b, g1, NCTX, NLAT)
        h, top_idx, gates = _router(x, norm2_g[l].reshape(1, D_MODEL), sc2, sh2, router_w[l],
                                    router_b[l].reshape(1, N_EXPERTS))
        block_e, n_valid, slot_src, slot_dst = _dispatch(top_idx[:, :TOP_K])
        bgu = b_gate_up[l]
        y = _experts(h, block_e, n_valid, slot_src, slot_dst, w_gate_up,
                     bgu[:, 0::2].reshape(N_EXPERTS, 1, D_FF), bgu[:, 1::2].reshape(N_EXPERTS, 1, D_FF),
                     w_down, b_down[l].reshape(N_EXPERTS, 1, D_MODEL), l)
        x = _combine(x, y, gates, g2)
    y_prompt = _final_norm(x, final_g.reshape(1, D_MODEL), 0, NCTX).reshape(BATCH, SEQ, D_MODEL)
    y_sample = _final_norm(x, final_g.reshape(1, D_MODEL), NCTX, NLAT).reshape(DEC_BATCH, DEC_SEQ, D_MODEL)
    return (y_prompt, y_sample, jnp.stack(new_na_k, axis=1), jnp.stack(new_na_v, axis=1),
            jnp.stack(new_gqa_k, axis=1), jnp.stack(new_gqa_v, axis=1), jnp.stack(new_ret, axis=1))
```

```python
import functools

import jax
import jax.numpy as jnp
import numpy as np
from jax import lax
from jax.experimental import pallas as pl
from jax.experimental.pallas import tpu as pltpu

F32 = jnp.float32
BF16 = jnp.bfloat16

D_MODEL = 1024
BATCH = 32
SEQ = 256
DEPTH = 2
DEC_BATCH = 4
DEC_SEQ = 4096
PAST_LEN = 256
GRID_W = 64
HEAD_DIM = 64
H_NA = 4
H_RET = 4
H_GQA = 8
KV_GQA = 2
D_NA = H_NA * HEAD_DIM
D_RET = H_RET * HEAD_DIM
D_GQA = H_GQA * HEAD_DIM
D_KV = KV_GQA * HEAD_DIM
D_IN = 3 * D_NA + 4 * D_RET + D_GQA + 2 * D_KV
WIN_R = 8
WIN_C = 16
RET_CHUNK = 128
N_EXPERTS = 32
TOP_K = 4
D_FF = D_MODEL
SWIGLU_LIMIT = 7.0
SWIGLU_ALPHA = 1.702
MOE_BLOCK = 256
ROPE_THETA = 10000.0
EPS = 1e-6
ATTN_SCALE = HEAD_DIM ** -0.5

NCTX = BATCH * SEQ
NLAT = DEC_BATCH * DEC_SEQ
NTOK = NCTX + NLAT
N_COND = 8
NK = NTOK * TOP_K
N_MOE_BLOCKS = NK // MOE_BLOCK + N_EXPERTS
LANES = 128
NEG_BIG = -1e30

C_AQ, C_AK, C_AV = 0, D_NA, 2 * D_NA
C_BQ = 3 * D_NA
C_BK = C_BQ + D_RET
C_BV = C_BK + D_RET
C_BG = C_BV + D_RET
C_CQ = C_BG + D_RET
C_CK = C_CQ + D_GQA
C_CV = C_CK + D_KV

ROW_TILE = 512
NA_QROWS = 4
NA_BAND = 12
GQA_TQ = 128
VMEM_LIMIT = 56 * 1024 * 1024


def _cparams(n_axes):
    return pltpu.CompilerParams(dimension_semantics=("arbitrary",) * n_axes,
                                vmem_limit_bytes=VMEM_LIMIT)


def _mod_index(i, tile):
    nctx = NCTX // tile
    per_batch = DEC_SEQ // tile
    return jnp.where(i < nctx, 0, 1 + (i - nctx) // per_batch)


def _dot(a, b):
    return jnp.dot(a, b, preferred_element_type=F32)


def _dot_nt(a, b):
    return lax.dot_general(a, b, (((1,), (1,)), ((), ())), preferred_element_type=F32)


def _dot_tn(a, b):
    return lax.dot_general(a, b, (((0,), (0,)), ((), ())), preferred_element_type=F32)


def _mod_kernel(c_ref, w_ref, b_ref, o_ref):
    c = c_ref[...]
    s = c * jax.nn.sigmoid(c)
    o_ref[...] = jnp.dot(s, w_ref[...], preferred_element_type=F32,
                         precision=lax.Precision.HIGHEST) + b_ref[...]


def _modulation(cond, ada_w, ada_b):
    tn = 1536
    return pl.pallas_call(
        _mod_kernel, name="modulation",
        grid=(DEPTH, 6 * D_MODEL // tn),
        in_specs=[pl.BlockSpec((N_COND, D_MODEL), lambda l, j: (0, 0)),
                  pl.BlockSpec((None, D_MODEL, tn), lambda l, j: (l, 0, j)),
                  pl.BlockSpec((None, 1, tn), lambda l, j: (l, 0, j))],
        out_specs=pl.BlockSpec((None, N_COND, tn), lambda l, j: (l, 0, j)),
        out_shape=jax.ShapeDtypeStruct((DEPTH, N_COND, 6 * D_MODEL), F32),
        compiler_params=_cparams(2),
    )(cond, ada_w, ada_b.reshape(DEPTH, 1, 6 * D_MODEL))


def _rms_mod(x, g, sc, sh):
    ms = jnp.mean(x * x, axis=-1, keepdims=True)
    return (x * lax.rsqrt(ms + EPS) * g) * (1.0 + sc) + sh


def _two_head_rsqrt(x):
    sq = x * x
    left = lax.broadcasted_iota(jnp.int32, x.shape, 1) < HEAD_DIM
    s_left = jnp.sum(jnp.where(left, sq, 0.0), axis=-1, keepdims=True)
    s_right = jnp.sum(jnp.where(left, 0.0, sq), axis=-1, keepdims=True)
    ms = jnp.where(left, s_left, s_right) * (1.0 / HEAD_DIM)
    return lax.rsqrt(ms + EPS)


def _rope128(x, cos, sin_signed):
    first = (lax.broadcasted_iota(jnp.int32, x.shape, 1) & (HEAD_DIM // 2)) == 0
    rot = jnp.where(first, pltpu.roll(x, LANES - HEAD_DIM // 2, 1), pltpu.roll(x, HEAD_DIM // 2, 1))
    return x * cos + rot * sin_signed


def _softmax_pv(scores, values):
    m = functools.reduce(jnp.maximum, [jnp.max(s, axis=-1, keepdims=True) for s in scores])
    ps = [jnp.exp(s - m) for s in scores]
    denom = functools.reduce(lambda a, b: a + b, [jnp.sum(p, axis=-1, keepdims=True) for p in ps])
    o = functools.reduce(lambda a, b: a + b, [_dot(p.astype(BF16), v) for p, v in zip(ps, values)])
    return o / denom


def _head(x, h):
    return x[:, h * HEAD_DIM:(h + 1) * HEAD_DIM]


TOK_SUB = D_MODEL // LANES


def _store_token_tiles(ref, row0, x):
    n = x.shape[0]
    for s in range(TOK_SUB):
        ref[pl.ds(row0 + s, n, stride=TOK_SUB), :] = x[:, s * LANES:(s + 1) * LANES]


def _load_token_chunk(ref, row0, n, s):
    return ref[pl.ds(row0 + s, n, stride=TOK_SUB), :]


def _in_proj_kernel(x_ref, g_ref, sc_ref, sh_ref, w_ref, rc_ref, rs_ref, ac_ref, as_ref,
                    qg_ref, kg_ref, z_ref):
    hb = _rms_mod(x_ref[...], g_ref[...], sc_ref[...], sh_ref[...]).astype(BF16)

    def proj(c0, c1):
        return _dot(hb, w_ref[:, c0:c1])

    z_ref[:, C_AQ:C_AK] = proj(C_AQ, C_AK) * ATTN_SCALE
    z_ref[:, C_AK:C_BQ] = proj(C_AK, C_BQ)
    rc, rs = rc_ref[...], rs_ref[...]
    zq = proj(C_BQ, C_BK)
    zk = proj(C_BK, C_BV)
    for j in range(D_RET // LANES):
        sl = slice(j * LANES, (j + 1) * LANES)
        z_ref[:, C_BQ + j * LANES:C_BQ + (j + 1) * LANES] = _rope128(zq[:, sl], rc, rs)
        z_ref[:, C_BK + j * LANES:C_BK + (j + 1) * LANES] = _rope128(zk[:, sl], rc, rs) * ATTN_SCALE
    z_ref[:, C_BV:C_CQ] = proj(C_BV, C_CQ)
    ac, asn = ac_ref[...], as_ref[...]
    zc = proj(C_CQ, C_CK)
    for j in range(D_GQA // LANES):
        t = zc[:, j * LANES:(j + 1) * LANES]
        t = t * _two_head_rsqrt(t) * qg_ref[...]
        z_ref[:, C_CQ + j * LANES:C_CQ + (j + 1) * LANES] = _rope128(t, ac, asn) * ATTN_SCALE
    zkv = proj(C_CK, D_IN)
    t = zkv[:, :LANES]
    t = t * _two_head_rsqrt(t) * kg_ref[...]
    z_ref[:, C_CK:C_CV] = _rope128(t, ac, asn)
    z_ref[:, C_CV:D_IN] = zkv[:, LANES:]


def _in_proj(x, g, sc, sh, w_b, tabs, qg, kg):
    tm = ROW_TILE
    nctx = NCTX // tm
    per_seq = DEC_SEQ // tm

    def tab_idx(i):
        return (jnp.where(i < nctx, per_seq, (i - nctx) % per_seq), 0)

    row = lambda i: (i, 0)
    fixed = lambda i: (0, 0)
    mod = lambda i: (_mod_index(i, tm), 0, 0)
    tab_spec = pl.BlockSpec((tm, LANES), tab_idx)
    return pl.pallas_call(
        _in_proj_kernel, name="in_proj",
        grid=(NTOK // tm,),
        in_specs=[pl.BlockSpec((tm, D_MODEL), row),
                  pl.BlockSpec((1, D_MODEL), fixed),
                  pl.BlockSpec((None, 1, D_MODEL), mod),
                  pl.BlockSpec((None, 1, D_MODEL), mod),
                  pl.BlockSpec((D_MODEL, D_IN), fixed),
                  tab_spec, tab_spec, tab_spec, tab_spec,
                  pl.BlockSpec((1, LANES), fixed),
                  pl.BlockSpec((1, LANES), fixed)],
        out_specs=pl.BlockSpec((tm, D_IN), row),
        out_shape=jax.ShapeDtypeStruct((NTOK, D_IN), F32),
        compiler_params=_cparams(1),
    )(x, g, sc, sh, w_b, *tabs, qg, kg)


def _rope_tables():
    t = np.arange(DEC_SEQ)
    inv_ret = 1.0 / (ROPE_THETA ** np.linspace(0.0, 1.0, HEAD_DIM // 2, dtype=np.float32))
    ang_ret = t.astype(np.float32)[:, None] * inv_ret.astype(np.float32)
    n = HEAD_DIM // 4
    inv_ax = (ROPE_THETA ** (-np.arange(n, dtype=np.float32) / n)).astype(np.float32)
    row = (t // GRID_W).astype(np.float32)
    col = (t % GRID_W).astype(np.float32)
    ang_ax = np.concatenate([row[:, None] * inv_ax, col[:, None] * inv_ax], axis=-1)

    def tables(ang):
        ang = jnp.asarray(ang, F32)
        cos, sin = jnp.cos(ang), jnp.sin(ang)
        cos2 = jnp.tile(jnp.concatenate([cos, cos], axis=-1), (1, LANES // HEAD_DIM))
        sin2 = jnp.tile(jnp.concatenate([-sin, sin], axis=-1), (1, LANES // HEAD_DIM))
        cos2 = jnp.concatenate([cos2, jnp.ones((ROW_TILE, LANES), F32)], axis=0)
        sin2 = jnp.concatenate([sin2, jnp.zeros((ROW_TILE, LANES), F32)], axis=0)
        return cos2, sin2

    return (*tables(ang_ret), *tables(ang_ax))


def _ctx_attn_kernel(aq_ref, ak_ref, av_ref, cq0_ref, cq1_ref, ck_ref, cv_ref, oa_ref, oc_ref):
    aq = aq_ref[...].astype(BF16)
    ak = ak_ref[...].astype(BF16)
    av = av_ref[...].astype(BF16)
    outs = []
    for h in range(H_NA):
        s = _dot_nt(_head(aq, h), _head(ak, h))
        outs.append(_softmax_pv([s], [_head(av, h)]))
    oa_ref[...] = jnp.concatenate(outs, axis=-1)
    ck = ck_ref[...].astype(BF16)
    cv = cv_ref[...].astype(BF16)
    group = H_GQA // KV_GQA
    outs = []
    for g, cq_ref in enumerate((cq0_ref, cq1_ref)):
        cq = cq_ref[...].astype(BF16)
        qs = jnp.concatenate([_head(cq, j) for j in range(group)], axis=0)
        o = _softmax_pv([_dot_nt(qs, _head(ck, g))], [_head(cv, g)])
        outs += [o[j * SEQ:(j + 1) * SEQ] for j in range(group)]
    oc_ref[...] = jnp.concatenate(outs, axis=-1)


def _ctx_attention(z):
    col = lambda c, w: pl.BlockSpec((SEQ, w), lambda b: (b, c // w))
    return pl.pallas_call(
        _ctx_attn_kernel, name="ctx_attn",
        grid=(BATCH,),
        in_specs=[col(C_AQ, D_NA), col(C_AK, D_NA), col(C_AV, D_NA),
                  col(C_CQ, D_NA), col(C_CQ + D_NA, D_NA), col(C_CK, D_KV), col(C_CV, D_KV)],
        out_specs=[pl.BlockSpec((SEQ, D_NA), lambda b: (b, 0)),
                   pl.BlockSpec((SEQ, D_GQA), lambda b: (b, 0))],
        out_shape=[jax.ShapeDtypeStruct((NCTX, D_NA), F32),
                   jax.ShapeDtypeStruct((NCTX, D_GQA), F32)],
        compiler_params=_cparams(1),
    )(z, z, z, z, z, z, z)


def _lat_gqa_kernel(cq0_ref, cq1_ref, ck_ref, cv_ref, kc_ref, vc_ref, oc_ref):
    ck = ck_ref[...].astype(BF16)
    cv = cv_ref[...].astype(BF16)
    kc = kc_ref[...].astype(BF16)
    vc = vc_ref[...].astype(BF16)
    group = H_GQA // KV_GQA
    outs = []
    for g, cq_ref in enumerate((cq0_ref, cq1_ref)):
        cq = cq_ref[...].astype(BF16)
        qs = jnp.concatenate([_head(cq, j) for j in range(group)], axis=0)
        o = _softmax_pv([_dot_nt(qs, _head(kc, g)), _dot_nt(qs, _head(ck, g))],
                        [_head(vc, g), _head(cv, g)])
        outs += [o[j * GQA_TQ:(j + 1) * GQA_TQ] for j in range(group)]
    oc_ref[...] = jnp.concatenate(outs, axis=-1)


def _lat_gqa(z, cache_k, cache_v, layer):
    nq = DEC_SEQ // GQA_TQ
    q0 = NCTX // GQA_TQ
    s0 = NCTX // DEC_SEQ
    qspec = lambda c: pl.BlockSpec((GQA_TQ, D_NA), lambda b, i: (q0 + b * nq + i, c // D_NA))
    kvspec = lambda c: pl.BlockSpec((DEC_SEQ, D_KV), lambda b, i: (s0 + b, c // D_KV))
    cspec = pl.BlockSpec((None, None, PAST_LEN, D_KV), lambda b, i: (b, layer, 0, 0))
    return pl.pallas_call(
        _lat_gqa_kernel, name="lat_gqa",
        grid=(DEC_BATCH, nq),
        in_specs=[qspec(C_CQ), qspec(C_CQ + D_NA), kvspec(C_CK), kvspec(C_CV), cspec, cspec],
        out_specs=pl.BlockSpec((GQA_TQ, D_GQA), lambda b, i: (b * nq + i, 0)),
        out_shape=jax.ShapeDtypeStruct((NLAT, D_GQA), F32),
        compiler_params=_cparams(2),
    )(z, z, z, z, cache_k, cache_v)


def _na_band_start(i):
    return jnp.clip(NA_QROWS * i - WIN_R // 2, 0, GRID_W - NA_BAND)


def _lat_na_kernel(q_ref, k_ref, v_ref, kc_ref, vc_ref, bias_ref, o_ref):
    i = pl.program_id(1)
    start = pl.multiple_of(_na_band_start(i) * GRID_W, GRID_W)
    q = q_ref[...].astype(BF16)
    kb = k_ref[pl.ds(start, NA_BAND * GRID_W), :].astype(BF16)
    vb = v_ref[pl.ds(start, NA_BAND * GRID_W), :].astype(BF16)
    kc = kc_ref[...].astype(BF16)
    vc = vc_ref[...].astype(BF16)
    outs = []
    for h in range(H_NA):
        qh = _head(q, h)
        s_loc = _dot_nt(qh, _head(kb, h)) + bias_ref[h]
        s_ctx = _dot_nt(qh, _head(kc, h))
        outs.append(_softmax_pv([s_loc, s_ctx], [_head(vb, h), _head(vc, h)]))
    o_ref[...] = jnp.concatenate(outs, axis=-1)


def _na_bias_tables(rpb):
    nq, nk = NA_QROWS * GRID_W, NA_BAND * GRID_W
    rows = DEC_SEQ // GRID_W
    nblk = rows // NA_QROWS
    n_rb, n_cb = 2 * WIN_R - 1, 2 * WIN_C - 1
    row_sel = np.zeros((3, NA_QROWS, NA_BAND, n_rb), np.float32)
    row_ok = np.zeros((3, NA_QROWS, NA_BAND), bool)
    for kind, blk in enumerate((0, 1, nblk - 1)):
        bs = int(np.clip(NA_QROWS * blk - WIN_R // 2, 0, rows - NA_BAND))
        for jr in range(NA_QROWS):
            r = NA_QROWS * blk + jr
            rs = int(np.clip(r - WIN_R // 2, 0, rows - WIN_R))
            for bi in range(NA_BAND):
                ri = bs + bi
                if rs <= ri < rs + WIN_R:
                    row_ok[kind, jr, bi] = True
                    row_sel[kind, jr, bi, ri - r + WIN_R - 1] = 1.0
    c = np.arange(GRID_W)
    cs = np.clip(c - WIN_C // 2, 0, GRID_W - WIN_C)
    col_ok = (c[None, :] >= cs[:, None]) & (c[None, :] < cs[:, None] + WIN_C)
    col_sel = np.zeros((n_cb, GRID_W, GRID_W), np.float32)
    qc, kc = np.nonzero(col_ok)
    col_sel[kc - qc + WIN_C - 1, qc, kc] = 1.0
    hi = lax.Precision.HIGHEST
    by_row = jnp.einsum('kjba,hax->hkjbx', row_sel, rpb, precision=hi)
    bias = jnp.einsum('hkjbx,xcz->khjcbz', by_row, col_sel, precision=hi)
    ok = row_ok[:, None, :, None, :, None] & col_ok[None, None, None, :, None, :]
    return jnp.where(ok, bias, NEG_BIG).reshape(3, H_NA, nq, nk).astype(F32)


def _lat_na(z, cache_k, cache_v, bias, layer):
    nq_rows = NA_QROWS * GRID_W
    nblk = DEC_SEQ // nq_rows
    q0 = NCTX // nq_rows
    s0 = NCTX // DEC_SEQ
    kvspec = lambda c: pl.BlockSpec((DEC_SEQ, D_NA), lambda b, i: (s0 + b, c // D_NA))
    cspec = pl.BlockSpec((None, None, PAST_LEN, D_NA), lambda b, i: (b, layer, 0, 0))
    kind = lambda b, i: (jnp.where(i == 0, 0, jnp.where(i == nblk - 1, 2, 1)), 0, 0, 0)
    return pl.pallas_call(
        _lat_na_kernel, name="lat_na",
        grid=(DEC_BATCH, nblk),
        in_specs=[pl.BlockSpec((nq_rows, D_NA), lambda b, i: (q0 + b * nblk + i, 0)),
                  kvspec(C_AK), kvspec(C_AV), cspec, cspec,
                  pl.BlockSpec((None, H_NA, nq_rows, NA_BAND * GRID_W), kind)],
        out_specs=pl.BlockSpec((nq_rows, D_NA), lambda b, i: (b * nblk + i, 0)),
        out_shape=jax.ShapeDtypeStruct((NLAT, D_NA), F32),
        compiler_params=_cparams(2),
    )(z, z, z, cache_k, cache_v, bias)


def _ret_kernel(lg_ref, q_ref, k_ref, v_ref, s0_ref, lgl_ref, o_ref, sfin_ref, s_scr, *, n_chunks):
    d = pl.program_id(1)
    c = pl.program_id(2)
    C = RET_CHUNK

    @pl.when(c == 0)
    def _():
        s_scr[...] = s0_ref[...].astype(F32)

    fwd = d == 0
    q = q_ref[...]
    k = k_ref[...]
    vb = v_ref[...].astype(BF16)
    row = lax.broadcasted_iota(jnp.int32, (C, 1), 0).astype(F32)
    pos_q = jnp.where(fwd, row + 1.0, C - row)
    pos_k = jnp.where(fwd, C - 1.0 - row, row)
    lgl = lgl_ref[...]
    q_decay = jnp.exp(lgl * pos_q)
    kd = (k * jnp.exp(lgl * pos_k)).astype(BF16)
    chunk_decay = jnp.exp(lgl * float(C))
    qb = q.astype(BF16)
    kb = k.astype(BF16)
    ii = lax.broadcasted_iota(jnp.int32, (C, C), 0)
    jj = lax.broadcasted_iota(jnp.int32, (C, C), 1)
    delta = jnp.where(fwd, ii - jj, jj - ii).astype(F32)
    outs = []
    for h in range(H_RET):
        lg = lg_ref[d, h]
        mask = jnp.where(delta >= 0.0, jnp.exp(lg * jnp.maximum(delta, 0.0)), 0.0)
        qh, kh, vh = _head(qb, h), _head(kb, h), _head(vb, h)
        inner = _dot((_dot_nt(qh, kh) * mask).astype(BF16), vh)
        s = s_scr[h]
        cross = _dot(qh, s.astype(BF16)) * _head(q_decay, h)
        s_scr[h] = s * _head(chunk_decay, h) + _dot_tn(_head(kd, h), vh)
        outs.append(inner + cross)
    o_ref[...] = jnp.concatenate(outs, axis=-1)

    @pl.when(c == n_chunks - 1)
    def _():
        sfin_ref[...] = s_scr[...]


def _retention(z, s0, log_g, row0, n_batch, seq):
    n = seq // RET_CHUNK
    r0 = row0 // RET_CHUNK
    lgl = jnp.repeat(log_g, HEAD_DIM, axis=-1).reshape(2, 1, D_RET)

    def chunk(b, d, c):
        return r0 + b * n + jnp.where(d == 0, c, n - 1 - c)

    zspec = lambda col: pl.BlockSpec((RET_CHUNK, D_RET), lambda b, d, c, lg: (chunk(b, d, c), col // D_RET))
    sspec = pl.BlockSpec((None, None, H_RET, HEAD_DIM, HEAD_DIM), lambda b, d, c, lg: (b, d, 0, 0, 0))
    return pl.pallas_call(
        functools.partial(_ret_kernel, n_chunks=n), name="retention",
        grid_spec=pltpu.PrefetchScalarGridSpec(
            num_scalar_prefetch=1,
            grid=(n_batch, 2, n),
            in_specs=[zspec(C_BQ), zspec(C_BK), zspec(C_BV), sspec,
                      pl.BlockSpec((None, 1, D_RET), lambda b, d, c, lg: (d, 0, 0))],
            out_specs=[pl.BlockSpec((None, RET_CHUNK, D_RET),
                                    lambda b, d, c, lg: (d, chunk(b, d, c) - r0, 0)),
                       sspec],
            scratch_shapes=[pltpu.VMEM((H_RET, HEAD_DIM, HEAD_DIM), F32)]),
        out_shape=[jax.ShapeDtypeStruct((2, n_batch * seq, D_RET), F32),
                   jax.ShapeDtypeStruct((n_batch, 2, H_RET, HEAD_DIM, HEAD_DIM), F32)],
        compiler_params=_cparams(3),
    )(log_g, z, z, z, s0, lgl)


def _out_proj_kernel(x_ref, oa_ref, of_ref, ob_ref, bg_ref, oc_ref, w_ref, gate_ref, y_ref):
    ob = of_ref[...] + ob_ref[...]
    bg = bg_ref[...]
    parts = []
    for j in range(D_RET // LANES):
        t = ob[:, j * LANES:(j + 1) * LANES]
        g = bg[:, j * LANES:(j + 1) * LANES]
        parts.append(t * _two_head_rsqrt(t) * (g * jax.nn.sigmoid(g)))
    obn = jnp.concatenate(parts, axis=-1).astype(BF16)
    o = (_dot(oa_ref[...].astype(BF16), w_ref[0:D_NA, :])
         + _dot(obn, w_ref[D_NA:D_NA + D_RET, :])
         + _dot(oc_ref[...].astype(BF16), w_ref[D_NA + D_RET:, :]))
    y_ref[...] = x_ref[...] + gate_ref[...] * o


def _out_proj(x, oa, o_ret, z, oc, w_b, gate, row0, rows):
    tm = ROW_TILE
    r0 = row0 // tm
    loc = lambda i: (i, 0)
    return pl.pallas_call(
        _out_proj_kernel, name="out_proj",
        grid=(rows // tm,),
        in_specs=[pl.BlockSpec((tm, D_MODEL), lambda i: (r0 + i, 0)),
                  pl.BlockSpec((tm, D_NA), loc),
                  pl.BlockSpec((None, tm, D_RET), lambda i: (0, i, 0)),
                  pl.BlockSpec((None, tm, D_RET), lambda i: (1, i, 0)),
                  pl.BlockSpec((tm, D_RET), lambda i: (r0 + i, C_BG // D_RET)),
                  pl.BlockSpec((tm, D_GQA), loc),
                  pl.BlockSpec((D_MODEL, D_MODEL), lambda i: (0, 0)),
                  pl.BlockSpec((None, 1, D_MODEL), lambda i: (_mod_index(r0 + i, tm), 0, 0))],
        out_specs=pl.BlockSpec((tm, D_MODEL), lambda i: (r0 + i, 0)),
        out_shape=jax.ShapeDtypeStruct((NTOK, D_MODEL), F32),
        input_output_aliases={0: 0},
        compiler_params=_cparams(1),
    )(x, oa, o_ret, o_ret, z, oc, w_b, gate)


def _router_kernel(x_ref, g_ref, sc_ref, sh_ref, w_ref, b_ref, h_ref, idx_ref, gate_ref):
    h = _rms_mod(x_ref[...], g_ref[...], sc_ref[...], sh_ref[...])
    _store_token_tiles(h_ref, 0, h)
    w = w_ref[...]
    w_hi = w.astype(BF16)
    w_lo = (w - w_hi.astype(F32)).astype(BF16)
    h_hi = h.astype(BF16)
    h_lo = (h - h_hi.astype(F32)).astype(BF16)
    logits = _dot(h_hi, w_hi) + (_dot(h_hi, w_lo) + _dot(h_lo, w_hi)) + b_ref[...]
    lane = lax.broadcasted_iota(jnp.int32, logits.shape, 1)
    out_lane = lax.broadcasted_iota(jnp.int32, idx_ref.shape, 1)
    idx_out = jnp.zeros(idx_ref.shape, jnp.int32)
    val_out = jnp.zeros(gate_ref.shape, F32)
    top = None
    denom = None
    for k in range(TOP_K):
        m = jnp.max(logits, axis=-1, keepdims=True)
        i = jnp.min(jnp.where(logits == m, lane, N_EXPERTS), axis=-1, keepdims=True)
        logits = jnp.where(lane == i, -jnp.inf, logits)
        if k == 0:
            top = m
        e = jnp.exp(m - top)
        denom = e if k == 0 else denom + e
        idx_out = jnp.where(out_lane == k, i, idx_out)
        val_out = jnp.where(out_lane == k, e, val_out)
    idx_ref[...] = idx_out
    gate_ref[...] = val_out / denom


def _router(x, g, sc, sh, rw, rb):
    tm = ROW_TILE
    row = lambda i: (i, 0)
    fixed = lambda i: (0, 0)
    mod = lambda i: (_mod_index(i, tm), 0, 0)
    return pl.pallas_call(
        _router_kernel, name="router",
        grid=(NTOK // tm,),
        in_specs=[pl.BlockSpec((tm, D_MODEL), row),
                  pl.BlockSpec((1, D_MODEL), fixed),
                  pl.BlockSpec((None, 1, D_MODEL), mod),
                  pl.BlockSpec((None, 1, D_MODEL), mod),
                  pl.BlockSpec((D_MODEL, N_EXPERTS), fixed),
                  pl.BlockSpec((1, N_EXPERTS), fixed)],
        out_specs=[pl.BlockSpec((tm * TOK_SUB, LANES), row),
                   pl.BlockSpec((tm, LANES), row),
                   pl.BlockSpec((tm, LANES), row)],
        out_shape=[jax.ShapeDtypeStruct((NTOK * TOK_SUB, LANES), F32),
                   jax.ShapeDtypeStruct((NTOK, LANES), jnp.int32),
                   jax.ShapeDtypeStruct((NTOK, LANES), F32)],
        compiler_params=_cparams(1),
    )(x, g, sc, sh, rw, rb)


PAD_ROW0 = TOP_K * NTOK
Y_ROWS = PAD_ROW0 + 2 * MOE_BLOCK
GU_CHUNK = 2 * LANES


def _expert_kernel(be_ref, nv_ref, src0_ref, srcn_ref, dst_ref, h_hbm, wgu_ref, bg_ref, bl_ref, wd_ref,
                   bd_ref, y_hbm, xbuf, ybuf, wg_s, wl_s, wd_s, t_scr, gsem, ssem):
    j = pl.program_id(0)
    nv = nv_ref[0]
    cur = j % 2
    nxt = 1 - cur
    half = MOE_BLOCK * TOK_SUB

    def tile(ref, t):
        start = t * TOK_SUB
        if not isinstance(start, int):
            start = pl.multiple_of(start, TOK_SUB)
        return ref.at[pl.ds(start, TOK_SUB), :]

    def issue_gather(src_ref, slot):
        for r in range(MOE_BLOCK):
            pltpu.make_async_copy(tile(h_hbm, src_ref[0, r]), tile(xbuf, slot * MOE_BLOCK + r),
                                  gsem.at[slot]).start()

    def issue_scatter(slot, dst_of):
        for r in range(MOE_BLOCK):
            pltpu.make_async_copy(tile(ybuf, slot * MOE_BLOCK + r), tile(y_hbm, dst_of(r)),
                                  ssem.at[slot]).start()

    def wait_half(buf, sem, slot):
        whole = buf.at[pl.ds(pl.multiple_of(slot * half, half), half), :]
        pltpu.make_async_copy(whole, whole, sem.at[slot]).wait()

    wait_gather = functools.partial(wait_half, xbuf, gsem)
    wait_scatter = functools.partial(wait_half, ybuf, ssem)

    @pl.when(j == 0)
    def _():
        issue_gather(src0_ref, 0)
        ybuf[...] = jnp.zeros(ybuf.shape, F32)
        for slot in range(2):
            issue_scatter(slot, lambda r, slot=slot: PAD_ROW0 + slot * MOE_BLOCK + r)

    changed = jnp.logical_or(j == 0, be_ref[j] != be_ref[jnp.maximum(j - 1, 0)])

    @pl.when(jnp.logical_and(j < nv, changed))
    def _():
        for c in range(2 * D_FF // GU_CHUNK):
            t = wgu_ref[:, c * GU_CHUNK:(c + 1) * GU_CHUNK].T
            for kk in range(D_MODEL // LANES):
                t_scr[kk] = t[:, kk * LANES:(kk + 1) * LANES]
            rows = slice(c * GU_CHUNK // 2, (c + 1) * GU_CHUNK // 2)
            for kk in range(D_MODEL // LANES):
                cols = slice(kk * LANES, (kk + 1) * LANES)
                wg_s[rows, cols] = t_scr[kk, pl.ds(0, GU_CHUNK // 2, stride=2), :].astype(BF16)
                wl_s[rows, cols] = t_scr[kk, pl.ds(1, GU_CHUNK // 2, stride=2), :].astype(BF16)
        wd_s[...] = wd_ref[...].astype(BF16)

    @pl.when(j < nv)
    def _():
        wait_scatter(cur)
        wait_gather(cur)
        issue_gather(srcn_ref, nxt)
        base = pl.multiple_of(cur * half, half)
        xb = jnp.concatenate([_load_token_chunk(xbuf, base, MOE_BLOCK, s).astype(BF16)
                              for s in range(TOK_SUB)], axis=-1)
        glu = jnp.minimum(_dot_nt(xb, wg_s[...]) + bg_ref[...], SWIGLU_LIMIT)
        lin = jnp.clip(_dot_nt(xb, wl_s[...]) + bl_ref[...], -SWIGLU_LIMIT, SWIGLU_LIMIT)
        act = glu * jax.nn.sigmoid(SWIGLU_ALPHA * glu) * (lin + 1.0)
        _store_token_tiles(ybuf, base, _dot(act.astype(BF16), wd_s[...]) + bd_ref[...])
        issue_scatter(cur, lambda r: dst_ref[0, r])

    @pl.when(j == nv - 1)
    def _():
        wait_gather(nxt)
        wait_scatter(nxt)
        wait_scatter(cur)


def _experts(h, block_e, n_valid, slot_src, slot_dst, wgu, bg, bl, wd, bd, layer):
    wmap = lambda j, be, nv: (be[j], 0, 0)
    wmap_l = lambda j, be, nv: (layer, be[j], 0, 0)
    idx_spec = lambda fn: pl.BlockSpec((None, 1, MOE_BLOCK), fn, memory_space=pltpu.SMEM)
    return pl.pallas_call(
        _expert_kernel,
        grid_spec=pltpu.PrefetchScalarGridSpec(
            num_scalar_prefetch=2,
            grid=(N_MOE_BLOCKS,),
            in_specs=[idx_spec(lambda j, be, nv: (0, 0, 0)),
                      idx_spec(lambda j, be, nv: (jnp.minimum(j + 1, N_MOE_BLOCKS - 1), 0, 0)),
                      idx_spec(lambda j, be, nv: (j, 0, 0)),
                      pl.BlockSpec(memory_space=pl.ANY),
                      pl.BlockSpec((None, None, D_MODEL, 2 * D_FF), wmap_l),
                      pl.BlockSpec((None, 1, D_FF), wmap),
                      pl.BlockSpec((None, 1, D_FF), wmap),
                      pl.BlockSpec((None, None, D_FF, D_MODEL), wmap_l),
                      pl.BlockSpec((None, 1, D_MODEL), wmap)],
            out_specs=pl.BlockSpec(memory_space=pl.ANY),
            scratch_shapes=[pltpu.VMEM((2 * MOE_BLOCK * TOK_SUB, LANES), F32),
                            pltpu.VMEM((2 * MOE_BLOCK * TOK_SUB, LANES), F32),
                            pltpu.VMEM((D_FF, D_MODEL), BF16),
                            pltpu.VMEM((D_FF, D_MODEL), BF16),
                            pltpu.VMEM((D_FF, D_MODEL), BF16),
                            pltpu.VMEM((D_MODEL // LANES, GU_CHUNK, LANES), F32),
                            pltpu.SemaphoreType.DMA((2,)),
                            pltpu.SemaphoreType.DMA((2,))]),
        out_shape=jax.ShapeDtypeStruct((Y_ROWS * TOK_SUB, LANES), F32),
        compiler_params=_cparams(1),
        name="experts",
    )(block_e, n_valid, slot_src, slot_src, slot_dst, h, wgu, bg, bl, wd, bd)


def _dispatch(top_idx):
    flat_e = top_idx.reshape(-1)
    order = jnp.argsort(flat_e, stable=True).astype(jnp.int32)
    experts = jnp.arange(N_EXPERTS, dtype=jnp.int32)
    counts = jnp.sum((flat_e[:, None] == experts[None, :]).astype(jnp.int32), axis=0)
    padded = ((counts + MOE_BLOCK - 1) // MOE_BLOCK) * MOE_BLOCK
    pad_end = jnp.cumsum(padded)
    pad_start = pad_end - padded
    start = jnp.cumsum(counts) - counts
    block0 = jnp.arange(N_MOE_BLOCKS, dtype=jnp.int32) * MOE_BLOCK
    n_valid = (pad_end[-1] // MOE_BLOCK).astype(jnp.int32).reshape(1)
    used = block0 < pad_end[-1]
    block_e = jnp.sum((pad_end[None, :] <= block0[:, None]).astype(jnp.int32), axis=1)
    block_e = jnp.minimum(block_e, N_EXPERTS - 1)
    onehot = block_e[:, None] == experts[None, :]
    pick = lambda v: jnp.sum(jnp.where(onehot, v[None, :], 0), axis=1)
    offset = block0 - pick(pad_start)
    first = pick(start) + offset
    left = jnp.where(used, pick(counts) - offset, 0)
    r = jnp.arange(MOE_BLOCK, dtype=jnp.int32)[None, :]
    real = r < left[:, None]
    pos = jnp.clip(first[:, None] + r, 0, NK - 1)
    aid = jnp.take(order, pos, axis=0)
    pad_dst = PAD_ROW0 + ((block0 // MOE_BLOCK) % 2)[:, None] * MOE_BLOCK + r
    slot_src = jnp.where(real, aid // TOP_K, 0)
    slot_dst = jnp.where(real, (aid % TOP_K) * NTOK + aid // TOP_K, pad_dst)
    last_e = jnp.sum(jnp.where(block0 == (n_valid[0] - 1) * MOE_BLOCK, block_e, 0))
    block_e = jnp.where(used, block_e, last_e).astype(jnp.int32)
    shape = (N_MOE_BLOCKS, 1, MOE_BLOCK)
    return block_e, n_valid, slot_src.reshape(shape), slot_dst.reshape(shape)


def _combine_kernel(x_ref, *refs):
    y_refs, (gate_ref, mod_ref, o_ref) = refs[:TOP_K], refs[TOP_K:]
    gates = gate_ref[...]
    n = x_ref.shape[0]
    for s in range(TOK_SUB):
        cols = slice(s * LANES, (s + 1) * LANES)
        acc = gates[:, 0:1] * _load_token_chunk(y_refs[0], 0, n, s)
        for k in range(1, TOP_K):
            acc = acc + gates[:, k:k + 1] * _load_token_chunk(y_refs[k], 0, n, s)
        o_ref[:, cols] = x_ref[:, cols] + mod_ref[:, cols] * acc


def _combine(x, y, gates, gate2):
    tm = ROW_TILE
    row = lambda i: (i, 0)
    plane = lambda k: pl.BlockSpec((tm * TOK_SUB, LANES), lambda i: (k * (NTOK // tm) + i, 0))
    return pl.pallas_call(
        _combine_kernel,
        grid=(NTOK // tm,),
        in_specs=[pl.BlockSpec((tm, D_MODEL), row)] + [plane(k) for k in range(TOP_K)]
        + [pl.BlockSpec((tm, LANES), row),
           pl.BlockSpec((None, 1, D_MODEL), lambda i: (_mod_index(i, tm), 0, 0))],
        out_specs=pl.BlockSpec((tm, D_MODEL), row),
        out_shape=jax.ShapeDtypeStruct((NTOK, D_MODEL), F32),
        input_output_aliases={0: 0},
        compiler_params=_cparams(1),
        name="combine",
    )(x, *([y] * TOP_K), gates, gate2)


def _final_norm_kernel(x_ref, g_ref, o_ref):
    x = x_ref[...]
    ms = jnp.mean(x * x, axis=-1, keepdims=True)
    o_ref[...] = x * lax.rsqrt(ms + EPS) * g_ref[...]


def _final_norm(x, g, row0, rows):
    tm = ROW_TILE
    r0 = row0 // tm
    return pl.pallas_call(
        _final_norm_kernel, name="final_norm",
        grid=(rows // tm,),
        in_specs=[pl.BlockSpec((tm, D_MODEL), lambda i: (r0 + i, 0)),
                  pl.BlockSpec((1, D_MODEL), lambda i: (0, 0))],
        out_specs=pl.BlockSpec((tm, D_MODEL), lambda i: (i, 0)),
        out_shape=jax.ShapeDtypeStruct((rows, D_MODEL), F32),
        compiler_params=_cparams(1),
    )(x, g)


def kernel(x_prompt, x_sample, cache_na_k, cache_na_v, cache_gqa_k, cache_gqa_v, state_ret, c, c_ctx,
           ada_w, ada_b, norm1_g, norm2_g, w_in, w_out, na_rpb, ret_decay_logit, q_norm_g, k_norm_g,
           router_w, router_b, w_gate_up, b_gate_up, w_down, b_down, final_g):
    x = jnp.concatenate([x_prompt.reshape(NCTX, D_MODEL), x_sample.reshape(NLAT, D_MODEL)], axis=0)
    cond = jnp.zeros((N_COND, D_MODEL), F32).at[0].set(c_ctx).at[1:1 + DEC_BATCH].set(c)
    mod = _modulation(cond, ada_w, ada_b)
    tabs = _rope_tables()
    cache_na_k = cache_na_k.reshape(DEC_BATCH, DEPTH, PAST_LEN, D_NA)
    cache_na_v = cache_na_v.reshape(DEC_BATCH, DEPTH, PAST_LEN, D_NA)
    cache_gqa_k = cache_gqa_k.reshape(DEC_BATCH, DEPTH, PAST_LEN, D_KV)
    cache_gqa_v = cache_gqa_v.reshape(DEC_BATCH, DEPTH, PAST_LEN, D_KV)
    zero_state = jnp.zeros((BATCH, 2, H_RET, HEAD_DIM, HEAD_DIM), F32)
    two_heads = lambda g: jnp.tile(g, LANES // HEAD_DIM).reshape(1, LANES)
    new_na_k, new_na_v, new_gqa_k, new_gqa_v, new_ret = [], [], [], [], []
    for l in range(DEPTH):
        sh1, sc1, g1, sh2, sc2, g2 = [m.reshape(N_COND, 1, D_MODEL) for m in jnp.split(mod[l], 6, axis=-1)]
        z = _in_proj(x, norm1_g[l].reshape(1, D_MODEL), sc1, sh1, w_in[l].astype(BF16), tabs,
                     two_heads(q_norm_g[l]), two_heads(k_norm_g[l]))
        log_g = jax.nn.log_sigmoid(ret_decay_logit[l].astype(F32))
        w_out_b = w_out[l].astype(BF16)
        oa_c, oc_c = _ctx_attention(z)
        oret_c, st = _retention(z, zero_state, log_g, 0, BATCH, SEQ)
        x = _out_proj(x, oa_c, oret_c, z, oc_c, w_out_b, g1, 0, NCTX)
        zc = z[:NCTX]
        new_na_k.append(zc[:, C_AK:C_AV].reshape(BATCH, SEQ, H_NA, HEAD_DIM))
        new_na_v.append(zc[:, C_AV:C_BQ].reshape(BATCH, SEQ, H_NA, HEAD_DIM))
        new_gqa_k.append(zc[:, C_CK:C_CV].reshape(BATCH, SEQ, KV_GQA, HEAD_DIM))
        new_gqa_v.append(zc[:, C_CV:D_IN].reshape(BATCH, SEQ, KV_GQA, HEAD_DIM))
        new_ret.append(st)
        oa_l = _lat_na(z, cache_na_k, cache_na_v, _na_bias_tables(na_rpb[l]), l)
        oc_l = _lat_gqa(z, cache_gqa_k, cache_gqa_v, l)
        oret_l, _ = _retention(z, state_ret[:, l], log_g, NCTX, DEC_BATCH, DEC_SEQ)
        x = _out_proj(x, oa_l, oret_l, z, oc_l, w_out_b, g1, NCTX, NLAT)
        h, top_idx, gates = _router(x, norm2_g[l].reshape(1, D_MODEL), sc2, sh2, router_w[l],
                                    router_b[l].reshape(1, N_EXPERTS))
        block_e, n_valid, slot_src, slot_dst = _dispatch(top_idx[:, :TOP_K])
        bgu = b_gate_up[l]
        y = _experts(h, block_e, n_valid, slot_src, slot_dst, w_gate_up,
                     bgu[:, 0::2].reshape(N_EXPERTS, 1, D_FF), bgu[:, 1::2].reshape(N_EXPERTS, 1, D_FF),
                     w_down, b_down[l].reshape(N_EXPERTS, 1, D_MODEL), l)
        x = _combine(x, y, gates, g2)
    y_prompt = _final_norm(x, final_g.reshape(1, D_MODEL), 0, NCTX).reshape(BATCH, SEQ, D_MODEL)
    y_sample = _final_norm(x, final_g.reshape(1, D_MODEL), NCTX, NLAT).reshape(DEC_BATCH, DEC_SEQ, D_MODEL)
    return (y_prompt, y_sample, jnp.stack(new_na_k, axis=1), jnp.stack(new_na_v, axis=1),
            jnp.stack(new_gqa_k, axis=1), jnp.stack(new_gqa_v, axis=1), jnp.stack(new_ret, axis=1))
```

```python
import functools

import jax
import jax.numpy as jnp
import numpy as np
from jax import lax
from jax.experimental import pallas as pl
from jax.experimental.pallas import tpu as pltpu

F32 = jnp.float32
BF16 = jnp.bfloat16

D_MODEL = 1024
BATCH = 32
SEQ = 256
DEPTH = 2
DEC_BATCH = 4
DEC_SEQ = 4096
PAST_LEN = 256
GRID_W = 64
HEAD_DIM = 64
H_NA = 4
H_RET = 4
H_GQA = 8
KV_GQA = 2
D_NA = H_NA * HEAD_DIM
D_RET = H_RET * HEAD_DIM
D_GQA = H_GQA * HEAD_DIM
D_KV = KV_GQA * HEAD_DIM
D_IN = 3 * D_NA + 4 * D_RET + D_GQA + 2 * D_KV
WIN_R = 8
WIN_C = 16
RET_CHUNK = 128
N_EXPERTS = 32
TOP_K = 4
D_FF = D_MODEL
SWIGLU_LIMIT = 7.0
SWIGLU_ALPHA = 1.702
MOE_BLOCK = 256
ROPE_THETA = 10000.0
EPS = 1e-6
ATTN_SCALE = HEAD_DIM ** -0.5

NCTX = BATCH * SEQ
NLAT = DEC_BATCH * DEC_SEQ
NTOK = NCTX + NLAT
N_COND = 8
NK = NTOK * TOP_K
N_MOE_BLOCKS = NK // MOE_BLOCK + N_EXPERTS
LANES = 128
NEG_BIG = -1e30

C_AQ, C_AK, C_AV = 0, D_NA, 2 * D_NA
C_BQ = 3 * D_NA
C_BK = C_BQ + D_RET
C_BV = C_BK + D_RET
C_BG = C_BV + D_RET
C_CQ = C_BG + D_RET
C_CK = C_CQ + D_GQA
C_CV = C_CK + D_KV

ROW_TILE = 512
NA_QROWS = 4
NA_BAND = 12
GQA_TQ = 128
VMEM_LIMIT = 56 * 1024 * 1024


def _cparams(n_axes):
    return pltpu.CompilerParams(dimension_semantics=("arbitrary",) * n_axes,
                                vmem_limit_bytes=VMEM_LIMIT)


def _mod_index(i, tile):
    nctx = NCTX // tile
    per_batch = DEC_SEQ // tile
    return jnp.where(i < nctx, 0, 1 + (i - nctx) // per_batch)


def _dot(a, b):
    return jnp.dot(a, b, preferred_element_type=F32)


def _dot_nt(a, b):
    return lax.dot_general(a, b, (((1,), (1,)), ((), ())), preferred_element_type=F32)


def _dot_tn(a, b):
    return lax.dot_general(a, b, (((0,), (0,)), ((), ())), preferred_element_type=F32)


def _mod_kernel(c_ref, w_ref, b_ref, o_ref):
    c = c_ref[...]
    s = c * jax.nn.sigmoid(c)
    o_ref[...] = jnp.dot(s, w_ref[...], preferred_element_type=F32,
                         precision=lax.Precision.HIGHEST) + b_ref[...]


def _modulation(cond, ada_w, ada_b):
    tn = 1536
    return pl.pallas_call(
        _mod_kernel, name="modulation",
        grid=(DEPTH, 6 * D_MODEL // tn),
        in_specs=[pl.BlockSpec((N_COND, D_MODEL), lambda l, j: (0, 0)),
                  pl.BlockSpec((None, D_MODEL, tn), lambda l, j: (l, 0, j)),
                  pl.BlockSpec((None, 1, tn), lambda l, j: (l, 0, j))],
        out_specs=pl.BlockSpec((None, N_COND, tn), lambda l, j: (l, 0, j)),
        out_shape=jax.ShapeDtypeStruct((DEPTH, N_COND, 6 * D_MODEL), F32),
        compiler_params=_cparams(2),
    )(cond, ada_w, ada_b.reshape(DEPTH, 1, 6 * D_MODEL))


def _rms_mod(x, g, sc, sh):
    ms = jnp.mean(x * x, axis=-1, keepdims=True)
    return (x * lax.rsqrt(ms + EPS) * g) * (1.0 + sc) + sh


def _two_head_rsqrt(x):
    sq = x * x
    left = lax.broadcasted_iota(jnp.int32, x.shape, 1) < HEAD_DIM
    s_left = jnp.sum(jnp.where(left, sq, 0.0), axis=-1, keepdims=True)
    s_right = jnp.sum(jnp.where(left, 0.0, sq), axis=-1, keepdims=True)
    ms = jnp.where(left, s_left, s_right) * (1.0 / HEAD_DIM)
    return lax.rsqrt(ms + EPS)


def _rope128(x, cos, sin_signed):
    first = (lax.broadcasted_iota(jnp.int32, x.shape, 1) & (HEAD_DIM // 2)) == 0
    rot = jnp.where(first, pltpu.roll(x, LANES - HEAD_DIM // 2, 1), pltpu.roll(x, HEAD_DIM // 2, 1))
    return x * cos + rot * sin_signed


def _softmax_pv(scores, values):
    m = functools.reduce(jnp.maximum, [jnp.max(s, axis=-1, keepdims=True) for s in scores])
    ps = [jnp.exp(s - m) for s in scores]
    denom = functools.reduce(lambda a, b: a + b, [jnp.sum(p, axis=-1, keepdims=True) for p in ps])
    o = functools.reduce(lambda a, b: a + b, [_dot(p.astype(BF16), v) for p, v in zip(ps, values)])
    return o / denom


def _head(x, h):
    return x[:, h * HEAD_DIM:(h + 1) * HEAD_DIM]


TOK_SUB = D_MODEL // LANES


def _store_token_tiles(ref, row0, x):
    n = x.shape[0]
    for s in range(TOK_SUB):
        ref[pl.ds(row0 + s, n, stride=TOK_SUB), :] = x[:, s * LANES:(s + 1) * LANES]


def _load_token_chunk(ref, row0, n, s):
    return ref[pl.ds(row0 + s, n, stride=TOK_SUB), :]


def _in_proj_kernel(x_ref, g_ref, sc_ref, sh_ref, w_ref, rc_ref, rs_ref, ac_ref, as_ref,
                    qg_ref, kg_ref, z_ref):
    hb = _rms_mod(x_ref[...], g_ref[...], sc_ref[...], sh_ref[...]).astype(BF16)

    def proj(c0, c1):
        return _dot(hb, w_ref[:, c0:c1])

    z_ref[:, C_AQ:C_AK] = proj(C_AQ, C_AK) * ATTN_SCALE
    z_ref[:, C_AK:C_BQ] = proj(C_AK, C_BQ)
    rc, rs = rc_ref[...], rs_ref[...]
    zq = proj(C_BQ, C_BK)
    zk = proj(C_BK, C_BV)
    for j in range(D_RET // LANES):
        sl = slice(j * LANES, (j + 1) * LANES)
        z_ref[:, C_BQ + j * LANES:C_BQ + (j + 1) * LANES] = _rope128(zq[:, sl], rc, rs)
        z_ref[:, C_BK + j * LANES:C_BK + (j + 1) * LANES] = _rope128(zk[:, sl], rc, rs) * ATTN_SCALE
    z_ref[:, C_BV:C_CQ] = proj(C_BV, C_CQ)
    ac, asn = ac_ref[...], as_ref[...]
    zc = proj(C_CQ, C_CK)
    for j in range(D_GQA // LANES):
        t = zc[:, j * LANES:(j + 1) * LANES]
        t = t * _two_head_rsqrt(t) * qg_ref[...]
        z_ref[:, C_CQ + j * LANES:C_CQ + (j + 1) * LANES] = _rope128(t, ac, asn) * ATTN_SCALE
    zkv = proj(C_CK, D_IN)
    t = zkv[:, :LANES]
    t = t * _two_head_rsqrt(t) * kg_ref[...]
    z_ref[:, C_CK:C_CV] = _rope128(t, ac, asn)
    z_ref[:, C_CV:D_IN] = zkv[:, LANES:]


def _in_proj(x, g, sc, sh, w_b, tabs, qg, kg):
    tm = ROW_TILE
    nctx = NCTX // tm
    per_seq = DEC_SEQ // tm

    def tab_idx(i):
        return (jnp.where(i < nctx, per_seq, (i - nctx) % per_seq), 0)

    row = lambda i: (i, 0)
    fixed = lambda i: (0, 0)
    mod = lambda i: (_mod_index(i, tm), 0, 0)
    tab_spec = pl.BlockSpec((tm, LANES), tab_idx)
    return pl.pallas_call(
        _in_proj_kernel, name="in_proj",
        grid=(NTOK // tm,),
        in_specs=[pl.BlockSpec((tm, D_MODEL), row),
                  pl.BlockSpec((1, D_MODEL), fixed),
                  pl.BlockSpec((None, 1, D_MODEL), mod),
                  pl.BlockSpec((None, 1, D_MODEL), mod),
                  pl.BlockSpec((D_MODEL, D_IN), fixed),
                  tab_spec, tab_spec, tab_spec, tab_spec,
                  pl.BlockSpec((1, LANES), fixed),
                  pl.BlockSpec((1, LANES), fixed)],
        out_specs=pl.BlockSpec((tm, D_IN), row),
        out_shape=jax.ShapeDtypeStruct((NTOK, D_IN), F32),
        compiler_params=_cparams(1),
    )(x, g, sc, sh, w_b, *tabs, qg, kg)


def _rope_tables():
    t = np.arange(DEC_SEQ)
    inv_ret = 1.0 / (ROPE_THETA ** np.linspace(0.0, 1.0, HEAD_DIM // 2, dtype=np.float32))
    ang_ret = t.astype(np.float32)[:, None] * inv_ret.astype(np.float32)
    n = HEAD_DIM // 4
    inv_ax = (ROPE_THETA ** (-np.arange(n, dtype=np.float32) / n)).astype(np.float32)
    row = (t // GRID_W).astype(np.float32)
    col = (t % GRID_W).astype(np.float32)
    ang_ax = np.concatenate([row[:, None] * inv_ax, col[:, None] * inv_ax], axis=-1)

    def tables(ang):
        ang = jnp.asarray(ang, F32)
        cos, sin = jnp.cos(ang), jnp.sin(ang)
        cos2 = jnp.tile(jnp.concatenate([cos, cos], axis=-1), (1, LANES // HEAD_DIM))
        sin2 = jnp.tile(jnp.concatenate([-sin, sin], axis=-1), (1, LANES // HEAD_DIM))
        cos2 = jnp.concatenate([cos2, jnp.ones((ROW_TILE, LANES), F32)], axis=0)
        sin2 = jnp.concatenate([sin2, jnp.zeros((ROW_TILE, LANES), F32)], axis=0)
        return cos2, sin2

    return (*tables(ang_ret), *tables(ang_ax))


def _ctx_attn_kernel(aq_ref, ak_ref, av_ref, cq0_ref, cq1_ref, ck_ref, cv_ref, oa_ref, oc_ref):
    aq = aq_ref[...].astype(BF16)
    ak = ak_ref[...].astype(BF16)
    av = av_ref[...].astype(BF16)
    outs = []
    for h in range(H_NA):
        s = _dot_nt(_head(aq, h), _head(ak, h))
        outs.append(_softmax_pv([s], [_head(av, h)]))
    oa_ref[...] = jnp.concatenate(outs, axis=-1)
    ck = ck_ref[...].astype(BF16)
    cv = cv_ref[...].astype(BF16)
    group = H_GQA // KV_GQA
    outs = []
    for g, cq_ref in enumerate((cq0_ref, cq1_ref)):
        cq = cq_ref[...].astype(BF16)
        qs = jnp.concatenate([_head(cq, j) for j in range(group)], axis=0)
        o = _softmax_pv([_dot_nt(qs, _head(ck, g))], [_head(cv, g)])
        outs += [o[j * SEQ:(j + 1) * SEQ] for j in range(group)]
    oc_ref[...] = jnp.concatenate(outs, axis=-1)


def _ctx_attention(z):
    col = lambda c, w: pl.BlockSpec((SEQ, w), lambda b: (b, c // w))
    return pl.pallas_call(
        _ctx_attn_kernel, name="ctx_attn",
        grid=(BATCH,),
        in_specs=[col(C_AQ, D_NA), col(C_AK, D_NA), col(C_AV, D_NA),
                  col(C_CQ, D_NA), col(C_CQ + D_NA, D_NA), col(C_CK, D_KV), col(C_CV, D_KV)],
        out_specs=[pl.BlockSpec((SEQ, D_NA), lambda b: (b, 0)),
                   pl.BlockSpec((SEQ, D_GQA), lambda b: (b, 0))],
        out_shape=[jax.ShapeDtypeStruct((NCTX, D_NA), F32),
                   jax.ShapeDtypeStruct((NCTX, D_GQA), F32)],
        compiler_params=_cparams(1),
    )(z, z, z, z, z, z, z)


def _lat_gqa_kernel(cq0_ref, cq1_ref, ck_ref, cv_ref, kc_ref, vc_ref, oc_ref):
    ck = ck_ref[...].astype(BF16)
    cv = cv_ref[...].astype(BF16)
    kc = kc_ref[...].astype(BF16)
    vc = vc_ref[...].astype(BF16)
    group = H_GQA // KV_GQA
    outs = []
    for g, cq_ref in enumerate((cq0_ref, cq1_ref)):
        cq = cq_ref[...].astype(BF16)
        qs = jnp.concatenate([_head(cq, j) for j in range(group)], axis=0)
        o = _softmax_pv([_dot_nt(qs, _head(kc, g)), _dot_nt(qs, _head(ck, g))],
                        [_head(vc, g), _head(cv, g)])
        outs += [o[j * GQA_TQ:(j + 1) * GQA_TQ] for j in range(group)]
    oc_ref[...] = jnp.concatenate(outs, axis=-1)


def _lat_gqa(z, cache_k, cache_v, layer):
    nq = DEC_SEQ // GQA_TQ
    q0 = NCTX // GQA_TQ
    s0 = NCTX // DEC_SEQ
    qspec = lambda c: pl.BlockSpec((GQA_TQ, D_NA), lambda b, i: (q0 + b * nq + i, c // D_NA))
    kvspec = lambda c: pl.BlockSpec((DEC_SEQ, D_KV), lambda b, i: (s0 + b, c // D_KV))
    cspec = pl.BlockSpec((None, None, PAST_LEN, D_KV), lambda b, i: (b, layer, 0, 0))
    return pl.pallas_call(
        _lat_gqa_kernel, name="lat_gqa",
        grid=(DEC_BATCH, nq),
        in_specs=[qspec(C_CQ), qspec(C_CQ + D_NA), kvspec(C_CK), kvspec(C_CV), cspec, cspec],
        out_specs=pl.BlockSpec((GQA_TQ, D_GQA), lambda b, i: (b * nq + i, 0)),
        out_shape=jax.ShapeDtypeStruct((NLAT, D_GQA), F32),
        compiler_params=_cparams(2),
    )(z, z, z, z, cache_k, cache_v)


def _na_band_start(i):
    return jnp.clip(NA_QROWS * i - WIN_R // 2, 0, GRID_W - NA_BAND)


def _lat_na_kernel(q_ref, k_ref, v_ref, kc_ref, vc_ref, bias_ref, o_ref):
    i = pl.program_id(1)
    start = pl.multiple_of(_na_band_start(i) * GRID_W, GRID_W)
    q = q_ref[...].astype(BF16)
    kb = k_ref[pl.ds(start, NA_BAND * GRID_W), :].astype(BF16)
    vb = v_ref[pl.ds(start, NA_BAND * GRID_W), :].astype(BF16)
    kc = kc_ref[...].astype(BF16)
    vc = vc_ref[...].astype(BF16)
    outs = []
    for h in range(H_NA):
        qh = _head(q, h)
        s_loc = _dot_nt(qh, _head(kb, h)) + bias_ref[h]
        s_ctx = _dot_nt(qh, _head(kc, h))
        outs.append(_softmax_pv([s_loc, s_ctx], [_head(vb, h), _head(vc, h)]))
    o_ref[...] = jnp.concatenate(outs, axis=-1)


def _na_bias_tables(rpb):
    nq, nk = NA_QROWS * GRID_W, NA_BAND * GRID_W
    rows = DEC_SEQ // GRID_W
    nblk = rows // NA_QROWS
    n_rb, n_cb = 2 * WIN_R - 1, 2 * WIN_C - 1
    row_sel = np.zeros((3, NA_QROWS, NA_BAND, n_rb), np.float32)
    row_ok = np.zeros((3, NA_QROWS, NA_BAND), bool)
    for kind, blk in enumerate((0, 1, nblk - 1)):
        bs = int(np.clip(NA_QROWS * blk - WIN_R // 2, 0, rows - NA_BAND))
        for jr in range(NA_QROWS):
            r = NA_QROWS * blk + jr
            rs = int(np.clip(r - WIN_R // 2, 0, rows - WIN_R))
            for bi in range(NA_BAND):
                ri = bs + bi
                if rs <= ri < rs + WIN_R:
                    row_ok[kind, jr, bi] = True
                    row_sel[kind, jr, bi, ri - r + WIN_R - 1] = 1.0
    c = np.arange(GRID_W)
    cs = np.clip(c - WIN_C // 2, 0, GRID_W - WIN_C)
    col_ok = (c[None, :] >= cs[:, None]) & (c[None, :] < cs[:, None] + WIN_C)
    col_sel = np.zeros((n_cb, GRID_W, GRID_W), np.float32)
    qc, kc = np.nonzero(col_ok)
    col_sel[kc - qc + WIN_C - 1, qc, kc] = 1.0
    hi = lax.Precision.HIGHEST
    by_row = jnp.einsum('kjba,hax->hkjbx', row_sel, rpb, precision=hi)
    bias = jnp.einsum('hkjbx,xcz->khjcbz', by_row, col_sel, precision=hi)
    ok = row_ok[:, None, :, None, :, None] & col_ok[None, None, None, :, None, :]
    return jnp.where(ok, bias, NEG_BIG).reshape(3, H_NA, nq, nk).astype(F32)


def _lat_na(z, cache_k, cache_v, bias, layer):
    nq_rows = NA_QROWS * GRID_W
    nblk = DEC_SEQ // nq_rows
    q0 = NCTX // nq_rows
    s0 = NCTX // DEC_SEQ
    kvspec = lambda c: pl.BlockSpec((DEC_SEQ, D_NA), lambda b, i: (s0 + b, c // D_NA))
    cspec = pl.BlockSpec((None, None, PAST_LEN, D_NA), lambda b, i: (b, layer, 0, 0))
    kind = lambda b, i: (jnp.where(i == 0, 0, jnp.where(i == nblk - 1, 2, 1)), 0, 0, 0)
    return pl.pallas_call(
        _lat_na_kernel, name="lat_na",
        grid=(DEC_BATCH, nblk),
        in_specs=[pl.BlockSpec((nq_rows, D_NA), lambda b, i: (q0 + b * nblk + i, 0)),
                  kvspec(C_AK), kvspec(C_AV), cspec, cspec,
                  pl.BlockSpec((None, H_NA, nq_rows, NA_BAND * GRID_W), kind)],
        out_specs=pl.BlockSpec((nq_rows, D_NA), lambda b, i: (b * nblk + i, 0)),
        out_shape=jax.ShapeDtypeStruct((NLAT, D_NA), F32),
        compiler_params=_cparams(2),
    )(z, z, z, cache_k, cache_v, bias)


def _ret_kernel(lg_ref, q_ref, k_ref, v_ref, s0_ref, lgl_ref, o_ref, sfin_ref, s_scr, *, n_chunks):
    d = pl.program_id(1)
    c = pl.program_id(2)
    C = RET_CHUNK

    @pl.when(c == 0)
    def _():
        s_scr[...] = s0_ref[...].astype(F32)

    fwd = d == 0
    q = q_ref[...]
    k = k_ref[...]
    vb = v_ref[...].astype(BF16)
    row = lax.broadcasted_iota(jnp.int32, (C, 1), 0).astype(F32)
    pos_q = jnp.where(fwd, row + 1.0, C - row)
    pos_k = jnp.where(fwd, C - 1.0 - row, row)
    lgl = lgl_ref[...]
    q_decay = jnp.exp(lgl * pos_q)
    kd = (k * jnp.exp(lgl * pos_k)).astype(BF16)
    chunk_decay = jnp.exp(lgl * float(C))
    qb = q.astype(BF16)
    kb = k.astype(BF16)
    ii = lax.broadcasted_iota(jnp.int32, (C, C), 0)
    jj = lax.broadcasted_iota(jnp.int32, (C, C), 1)
    delta = jnp.where(fwd, ii - jj, jj - ii).astype(F32)
    outs = []
    for h in range(H_RET):
        lg = lg_ref[d, h]
        mask = jnp.where(delta >= 0.0, jnp.exp(lg * jnp.maximum(delta, 0.0)), 0.0)
        qh, kh, vh = _head(qb, h), _head(kb, h), _head(vb, h)
        inner = _dot((_dot_nt(qh, kh) * mask).astype(BF16), vh)
        s = s_scr[h]
        cross = _dot(qh, s.astype(BF16)) * _head(q_decay, h)
        s_scr[h] = s * _head(chunk_decay, h) + _dot_tn(_head(kd, h), vh)
        outs.append(inner + cross)
    o_ref[...] = jnp.concatenate(outs, axis=-1)

    @pl.when(c == n_chunks - 1)
    def _():
        sfin_ref[...] = s_scr[...]


def _retention(z, s0, log_g, row0, n_batch, seq):
    n = seq // RET_CHUNK
    r0 = row0 // RET_CHUNK
    lgl = jnp.repeat(log_g, HEAD_DIM, axis=-1).reshape(2, 1, D_RET)

    def chunk(b, d, c):
        return r0 + b * n + jnp.where(d == 0, c, n - 1 - c)

    zspec = lambda col: pl.BlockSpec((RET_CHUNK, D_RET), lambda b, d, c, lg: (chunk(b, d, c), col // D_RET))
    sspec = pl.BlockSpec((None, None, H_RET, HEAD_DIM, HEAD_DIM), lambda b, d, c, lg: (b, d, 0, 0, 0))
    return pl.pallas_call(
        functools.partial(_ret_kernel, n_chunks=n), name="retention",
        grid_spec=pltpu.PrefetchScalarGridSpec(
            num_scalar_prefetch=1,
            grid=(n_batch, 2, n),
            in_specs=[zspec(C_BQ), zspec(C_BK), zspec(C_BV), sspec,
                      pl.BlockSpec((None, 1, D_RET), lambda b, d, c, lg: (d, 0, 0))],
            out_specs=[pl.BlockSpec((None, RET_CHUNK, D_RET),
                                    lambda b, d, c, lg: (d, chunk(b, d, c) - r0, 0)),
                       sspec],
            scratch_shapes=[pltpu.VMEM((H_RET, HEAD_DIM, HEAD_DIM), F32)]),
        out_shape=[jax.ShapeDtypeStruct((2, n_batch * seq, D_RET), F32),
                   jax.ShapeDtypeStruct((n_batch, 2, H_RET, HEAD_DIM, HEAD_DIM), F32)],
        compiler_params=_cparams(3),
    )(log_g, z, z, z, s0, lgl)


def _out_proj_kernel(x_ref, oa_ref, of_ref, ob_ref, bg_ref, oc_ref, w_ref, gate_ref, y_ref):
    ob = of_ref[...] + ob_ref[...]
    bg = bg_ref[...]
    parts = []
    for j in range(D_RET // LANES):
        t = ob[:, j * LANES:(j + 1) * LANES]
        g = bg[:, j * LANES:(j + 1) * LANES]
        parts.append(t * _two_head_rsqrt(t) * (g * jax.nn.sigmoid(g)))
    obn = jnp.concatenate(parts, axis=-1).astype(BF16)
    o = (_dot(oa_ref[...].astype(BF16), w_ref[0:D_NA, :])
         + _dot(obn, w_ref[D_NA:D_NA + D_RET, :])
         + _dot(oc_ref[...].astype(BF16), w_ref[D_NA + D_RET:, :]))
    y_ref[...] = x_ref[...] + gate_ref[...] * o


def _out_proj(x, oa, o_ret, z, oc, w_b, gate, row0, rows):
    tm = ROW_TILE
    r0 = row0 // tm
    loc = lambda i: (i, 0)
    return pl.pallas_call(
        _out_proj_kernel, name="out_proj",
        grid=(rows // tm,),
        in_specs=[pl.BlockSpec((tm, D_MODEL), lambda i: (r0 + i, 0)),
                  pl.BlockSpec((tm, D_NA), loc),
                  pl.BlockSpec((None, tm, D_RET), lambda i: (0, i, 0)),
                  pl.BlockSpec((None, tm, D_RET), lambda i: (1, i, 0)),
                  pl.BlockSpec((tm, D_RET), lambda i: (r0 + i, C_BG // D_RET)),
                  pl.BlockSpec((tm, D_GQA), loc),
                  pl.BlockSpec((D_MODEL, D_MODEL), lambda i: (0, 0)),
                  pl.BlockSpec((None, 1, D_MODEL), lambda i: (_mod_index(r0 + i, tm), 0, 0))],
        out_specs=pl.BlockSpec((tm, D_MODEL), lambda i: (r0 + i, 0)),
        out_shape=jax.ShapeDtypeStruct((NTOK, D_MODEL), F32),
        input_output_aliases={0: 0},
        compiler_params=_cparams(1),
    )(x, oa, o_ret, o_ret, z, oc, w_b, gate)


def _router_kernel(x_ref, g_ref, sc_ref, sh_ref, w_ref, b_ref, h_ref, idx_ref, gate_ref):
    h = _rms_mod(x_ref[...], g_ref[...], sc_ref[...], sh_ref[...])
    _store_token_tiles(h_ref, 0, h)
    w = w_ref[...]
    w_hi = w.astype(BF16)
    w_lo = (w - w_hi.astype(F32)).astype(BF16)
    h_hi = h.astype(BF16)
    h_lo = (h - h_hi.astype(F32)).astype(BF16)
    logits = _dot(h_hi, w_hi) + (_dot(h_hi, w_lo) + _dot(h_lo, w_hi)) + b_ref[...]
    lane = lax.broadcasted_iota(jnp.int32, logits.shape, 1)
    out_lane = lax.broadcasted_iota(jnp.int32, idx_ref.shape, 1)
    idx_out = jnp.zeros(idx_ref.shape, jnp.int32)
    val_out = jnp.zeros(gate_ref.shape, F32)
    top = None
    denom = None
    for k in range(TOP_K):
        m = jnp.max(logits, axis=-1, keepdims=True)
        i = jnp.min(jnp.where(logits == m, lane, N_EXPERTS), axis=-1, keepdims=True)
        logits = jnp.where(lane == i, -jnp.inf, logits)
        if k == 0:
            top = m
        e = jnp.exp(m - top)
        denom = e if k == 0 else denom + e
        idx_out = jnp.where(out_lane == k, i, idx_out)
        val_out = jnp.where(out_lane == k, e, val_out)
    idx_ref[...] = idx_out
    gate_ref[...] = val_out / denom


def _router(x, g, sc, sh, rw, rb):
    tm = ROW_TILE
    row = lambda i: (i, 0)
    fixed = lambda i: (0, 0)
    mod = lambda i: (_mod_index(i, tm), 0, 0)
    return pl.pallas_call(
        _router_kernel, name="router",
        grid=(NTOK // tm,),
        in_specs=[pl.BlockSpec((tm, D_MODEL), row),
                  pl.BlockSpec((1, D_MODEL), fixed),
                  pl.BlockSpec((None, 1, D_MODEL), mod),
                  pl.BlockSpec((None, 1, D_MODEL), mod),
                  pl.BlockSpec((D_MODEL, N_EXPERTS), fixed),
                  pl.BlockSpec((1, N_EXPERTS), fixed)],
        out_specs=[pl.BlockSpec((tm * TOK_SUB, LANES), row),
                   pl.BlockSpec((tm, LANES), row),
                   pl.BlockSpec((tm, LANES), row)],
        out_shape=[jax.ShapeDtypeStruct((NTOK * TOK_SUB, LANES), F32),
                   jax.ShapeDtypeStruct((NTOK, LANES), jnp.int32),
                   jax.ShapeDtypeStruct((NTOK, LANES), F32)],
        compiler_params=_cparams(1),
    )(x, g, sc, sh, rw, rb)


PAD_ROW0 = TOP_K * NTOK
Y_ROWS = PAD_ROW0 + 2 * MOE_BLOCK
GU_CHUNK = 2 * LANES


def _expert_kernel(be_ref, nv_ref, ne_ref, ws_ref, src0_ref, srcn_ref, dst_ref, h_hbm, wgu_hbm, bg_ref,
                   bl_ref, wd_hbm, bd_ref, y_hbm, xbuf, ybuf, wg_s, wl_s, wd_s, t_scr, wgu_buf, wd_buf,
                   gsem, ssem, wsem, *, layer):
    j = pl.program_id(0)
    nv = nv_ref[0]
    cur = j % 2
    nxt = 1 - cur
    half = MOE_BLOCK * TOK_SUB

    def tile(ref, t):
        start = t * TOK_SUB
        if not isinstance(start, int):
            start = pl.multiple_of(start, TOK_SUB)
        return ref.at[pl.ds(start, TOK_SUB), :]

    def issue_gather(src_ref, slot):
        for r in range(MOE_BLOCK):
            pltpu.make_async_copy(tile(h_hbm, src_ref[0, r]), tile(xbuf, slot * MOE_BLOCK + r),
                                  gsem.at[slot]).start()

    def issue_scatter(slot, dst_of):
        for r in range(MOE_BLOCK):
            pltpu.make_async_copy(tile(ybuf, slot * MOE_BLOCK + r), tile(y_hbm, dst_of(r)),
                                  ssem.at[slot]).start()

    def wait_half(buf, sem, slot):
        whole = buf.at[pl.ds(pl.multiple_of(slot * half, half), half), :]
        pltpu.make_async_copy(whole, whole, sem.at[slot]).wait()

    wait_gather = functools.partial(wait_half, xbuf, gsem)
    wait_scatter = functools.partial(wait_half, ybuf, ssem)

    def weight_copies(e, p):
        return (pltpu.make_async_copy(wgu_hbm.at[layer, e], wgu_buf.at[p], wsem.at[p, 0]),
                pltpu.make_async_copy(wd_hbm.at[layer, e], wd_buf.at[p], wsem.at[p, 1]))

    @pl.when(j == 0)
    def _():
        issue_gather(src0_ref, 0)
        ybuf[...] = jnp.zeros(ybuf.shape, F32)
        for slot in range(2):
            issue_scatter(slot, lambda r, slot=slot: PAD_ROW0 + slot * MOE_BLOCK + r)
        for cp in weight_copies(be_ref[0], 0):
            cp.start()

    changed =jnp.logical_or(j == 0, be_ref[j] != be_ref[jnp.maximum(j - 1, 0)])

    @pl.when(jnp.logical_and(j < nv, changed))
    def _():
        p = ws_ref[j]
        for cp in weight_copies(be_ref[j], p):
            cp.wait()

        @pl.when(ne_ref[j] >= 0)
        def _():
            for cp in weight_copies(ne_ref[j], 1 - p):
                cp.start()

        wgu_ref = wgu_buf.at[p]
        wd_ref = wd_buf.at[p]
        for c in range(2 * D_FF // GU_CHUNK):
            t = wgu_ref[:, c * GU_CHUNK:(c + 1) * GU_CHUNK].T
            for kk in range(D_MODEL // LANES):
                t_scr[kk] = t[:, kk * LANES:(kk + 1) * LANES]
            rows = slice(c * GU_CHUNK // 2, (c + 1) * GU_CHUNK // 2)
            for kk in range(D_MODEL // LANES):
                cols = slice(kk * LANES, (kk + 1) * LANES)
                wg_s[rows, cols] = t_scr[kk, pl.ds(0, GU_CHUNK // 2, stride=2), :].astype(BF16)
                wl_s[rows, cols] = t_scr[kk, pl.ds(1, GU_CHUNK // 2, stride=2), :].astype(BF16)
        wd_s[...] = wd_ref[...].astype(BF16)

    @pl.when(j < nv)
    def _():
        wait_scatter(cur)
        wait_gather(cur)
        issue_gather(srcn_ref, nxt)
        base = pl.multiple_of(cur * half, half)
        xb = jnp.concatenate([_load_token_chunk(xbuf, base, MOE_BLOCK, s).astype(BF16)
                              for s in range(TOK_SUB)], axis=-1)
        glu = jnp.minimum(_dot_nt(xb, wg_s[...]) + bg_ref[...], SWIGLU_LIMIT)
        lin = jnp.clip(_dot_nt(xb, wl_s[...]) + bl_ref[...], -SWIGLU_LIMIT, SWIGLU_LIMIT)
        act = glu * jax.nn.sigmoid(SWIGLU_ALPHA * glu) * (lin + 1.0)
        _store_token_tiles(ybuf, base, _dot(act.astype(BF16), wd_s[...]) + bd_ref[...])
        issue_scatter(cur, lambda r: dst_ref[0, r])

    @pl.when(j == nv - 1)
    def _():
        wait_gather(nxt)
        wait_scatter(nxt)
        wait_scatter(cur)


def _experts(h, plan, slot_src, slot_dst, wgu, bg, bl, wd, bd, layer):
    wmap = lambda j, be, *_: (be[j], 0, 0)
    idx_spec = lambda fn: pl.BlockSpec((None, 1, MOE_BLOCK), fn, memory_space=pltpu.SMEM)
    return pl.pallas_call(
        functools.partial(_expert_kernel, layer=layer),
        grid_spec=pltpu.PrefetchScalarGridSpec(
            num_scalar_prefetch=len(plan),
            grid=(N_MOE_BLOCKS,),
            in_specs=[idx_spec(lambda j, *_: (0, 0, 0)),
                      idx_spec(lambda j, *_: (jnp.minimum(j + 1, N_MOE_BLOCKS - 1), 0, 0)),
                      idx_spec(lambda j, *_: (j, 0, 0)),
                      pl.BlockSpec(memory_space=pl.ANY),
                      pl.BlockSpec(memory_space=pl.ANY),
                      pl.BlockSpec((None, 1, D_FF), wmap),
                      pl.BlockSpec((None, 1, D_FF), wmap),
                      pl.BlockSpec(memory_space=pl.ANY),
                      pl.BlockSpec((None, 1, D_MODEL), wmap)],
            out_specs=pl.BlockSpec(memory_space=pl.ANY),
            scratch_shapes=[pltpu.VMEM((2 * MOE_BLOCK * TOK_SUB, LANES), F32),
                            pltpu.VMEM((2 * MOE_BLOCK * TOK_SUB, LANES), F32),
                            pltpu.VMEM((D_FF, D_MODEL), BF16),
                            pltpu.VMEM((D_FF, D_MODEL), BF16),
                            pltpu.VMEM((D_FF, D_MODEL), BF16),
                            pltpu.VMEM((D_MODEL // LANES, GU_CHUNK, LANES), F32),
                            pltpu.VMEM((2, D_MODEL, 2 * D_FF), F32),
                            pltpu.VMEM((2, D_FF, D_MODEL), F32),
                            pltpu.SemaphoreType.DMA((2,)),
                            pltpu.SemaphoreType.DMA((2,)),
                            pltpu.SemaphoreType.DMA((2, 2))]),
        out_shape=jax.ShapeDtypeStruct((Y_ROWS * TOK_SUB, LANES), F32),
        compiler_params=_cparams(1),
        name="experts",
    )(*plan, slot_src, slot_src, slot_dst, h, wgu, bg, bl, wd, bd)


def _dispatch(top_idx):
    flat_e = top_idx.reshape(-1)
    order = jnp.argsort(flat_e, stable=True).astype(jnp.int32)
    experts = jnp.arange(N_EXPERTS, dtype=jnp.int32)
    counts = jnp.sum((flat_e[:, None] == experts[None, :]).astype(jnp.int32), axis=0)
    padded = ((counts + MOE_BLOCK - 1) // MOE_BLOCK) * MOE_BLOCK
    pad_end = jnp.cumsum(padded)
    pad_start = pad_end - padded
    start = jnp.cumsum(counts) - counts
    block0 = jnp.arange(N_MOE_BLOCKS, dtype=jnp.int32) * MOE_BLOCK
    n_valid = (pad_end[-1] // MOE_BLOCK).astype(jnp.int32).reshape(1)
    used = block0 < pad_end[-1]
    block_e = jnp.sum((pad_end[None, :] <= block0[:, None]).astype(jnp.int32), axis=1)
    block_e = jnp.minimum(block_e, N_EXPERTS - 1)
    onehot = block_e[:, None] == experts[None, :]
    pick = lambda v: jnp.sum(jnp.where(onehot, v[None, :], 0), axis=1)
    offset = block0 - pick(pad_start)
    first = pick(start) + offset
    left = jnp.where(used, pick(counts) - offset, 0)
    r = jnp.arange(MOE_BLOCK, dtype=jnp.int32)[None, :]
    real = r < left[:, None]
    pos = jnp.clip(first[:, None] + r, 0, NK - 1)
    aid = jnp.take(order, pos, axis=0)
    pad_dst = PAD_ROW0 + ((block0 // MOE_BLOCK) % 2)[:, None] * MOE_BLOCK + r
    slot_src = jnp.where(real, aid // TOP_K, 0)
    slot_dst = jnp.where(real, (aid % TOP_K) * NTOK + aid // TOP_K, pad_dst)
    run_first = jnp.logical_and(used, offset == 0)
    w_half = ((jnp.cumsum(run_first.astype(jnp.int32)) - 1) % 2).astype(jnp.int32)
    next_block = pick(pad_end) // MOE_BLOCK
    next_e = jnp.take(block_e, jnp.minimum(next_block, N_MOE_BLOCKS - 1), axis=0)
    next_e = jnp.where(jnp.logical_and(used, next_block < n_valid[0]), next_e, -1).astype(jnp.int32)
    last_e = jnp.sum(jnp.where(block0 == (n_valid[0] - 1) * MOE_BLOCK, block_e, 0))
    block_e = jnp.where(used, block_e, last_e).astype(jnp.int32)
    shape = (N_MOE_BLOCKS, 1, MOE_BLOCK)
    return (block_e, n_valid, next_e, w_half), slot_src.reshape(shape), slot_dst.reshape(shape)


def _combine_kernel(x_ref, *refs):
    y_refs, (gate_ref, mod_ref, o_ref) = refs[:TOP_K], refs[TOP_K:]
    gates = gate_ref[...]
    n = x_ref.shape[0]
    for s in range(TOK_SUB):
        cols = slice(s * LANES, (s + 1) * LANES)
        acc = gates[:, 0:1] * _load_token_chunk(y_refs[0], 0, n, s)
        for k in range(1, TOP_K):
            acc = acc + gates[:, k:k + 1] * _load_token_chunk(y_refs[k], 0, n, s)
        o_ref[:, cols] = x_ref[:, cols] + mod_ref[:, cols] * acc


def _combine(x, y, gates, gate2):
    tm = ROW_TILE
    row = lambda i: (i, 0)
    plane = lambda k: pl.BlockSpec((tm * TOK_SUB, LANES), lambda i: (k * (NTOK // tm) + i, 0))
    return pl.pallas_call(
        _combine_kernel,
        grid=(NTOK // tm,),
        in_specs=[pl.BlockSpec((tm, D_MODEL), row)] + [plane(k) for k in range(TOP_K)]
        + [pl.BlockSpec((tm, LANES), row),
           pl.BlockSpec((None, 1, D_MODEL), lambda i: (_mod_index(i, tm), 0, 0))],
        out_specs=pl.BlockSpec((tm, D_MODEL), row),
        out_shape=jax.ShapeDtypeStruct((NTOK, D_MODEL), F32),
        input_output_aliases={0: 0},
        compiler_params=_cparams(1),
        name="combine",
    )(x, *([y] * TOP_K), gates, gate2)


def _final_norm_kernel(x_ref, g_ref, o_ref):
    x = x_ref[...]
    ms = jnp.mean(x * x, axis=-1, keepdims=True)
    o_ref[...] = x * lax.rsqrt(ms + EPS) * g_ref[...]


def _final_norm(x, g, row0, rows):
    tm = ROW_TILE
    r0 = row0 // tm
    return pl.pallas_call(
        _final_norm_kernel, name="final_norm",
        grid=(rows // tm,),
        in_specs=[pl.BlockSpec((tm, D_MODEL), lambda i: (r0 + i, 0)),
                  pl.BlockSpec((1, D_MODEL), lambda i: (0, 0))],
        out_specs=pl.BlockSpec((tm, D_MODEL), lambda i: (i, 0)),
        out_shape=jax.ShapeDtypeStruct((rows, D_MODEL), F32),
        compiler_params=_cparams(1),
    )(x, g)


def kernel(x_prompt, x_sample, cache_na_k, cache_na_v, cache_gqa_k, cache_gqa_v, state_ret, c, c_ctx,
           ada_w, ada_b, norm1_g, norm2_g, w_in, w_out, na_rpb, ret_decay_logit, q_norm_g, k_norm_g,
           router_w, router_b, w_gate_up, b_gate_up, w_down, b_down, final_g):
    x = jnp.concatenate([x_prompt.reshape(NCTX, D_MODEL), x_sample.reshape(NLAT, D_MODEL)], axis=0)
    cond = jnp.zeros((N_COND, D_MODEL), F32).at[0].set(c_ctx).at[1:1 + DEC_BATCH].set(c)
    mod = _modulation(cond, ada_w, ada_b)
    tabs = _rope_tables()
    cache_na_k = cache_na_k.reshape(DEC_BATCH, DEPTH, PAST_LEN, D_NA)
    cache_na_v = cache_na_v.reshape(DEC_BATCH, DEPTH, PAST_LEN, D_NA)
    cache_gqa_k = cache_gqa_k.reshape(DEC_BATCH, DEPTH, PAST_LEN, D_KV)
    cache_gqa_v = cache_gqa_v.reshape(DEC_BATCH, DEPTH, PAST_LEN, D_KV)
    zero_state = jnp.zeros((BATCH, 2, H_RET, HEAD_DIM, HEAD_DIM), F32)
    two_heads = lambda g: jnp.tile(g, LANES // HEAD_DIM).reshape(1, LANES)
    new_na_k, new_na_v, new_gqa_k, new_gqa_v, new_ret = [], [], [], [], []
    for l in range(DEPTH):
        sh1, sc1, g1, sh2, sc2, g2 = [m.reshape(N_COND, 1, D_MODEL) for m in jnp.split(mod[l], 6, axis=-1)]
        z = _in_proj(x, norm1_g[l].reshape(1, D_MODEL), sc1, sh1, w_in[l].astype(BF16), tabs,
                     two_heads(q_norm_g[l]), two_heads(k_norm_g[l]))
        log_g = jax.nn.log_sigmoid(ret_decay_logit[l].astype(F32))
        w_out_b = w_out[l].astype(BF16)
        oa_c, oc_c = _ctx_attention(z)
        oret_c, st = _retention(z, zero_state, log_g, 0, BATCH, SEQ)
        x = _out_proj(x, oa_c, oret_c, z, oc_c, w_out_b, g1, 0, NCTX)
        zc = z[:NCTX]
        new_na_k.append(zc[:, C_AK:C_AV].reshape(BATCH, SEQ, H_NA, HEAD_DIM))
        new_na_v.append(zc[:, C_AV:C_BQ].reshape(BATCH, SEQ, H_NA, HEAD_DIM))
        new_gqa_k.append(zc[:, C_CK:C_CV].reshape(BATCH, SEQ, KV_GQA, HEAD_DIM))
        new_gqa_v.append(zc[:, C_CV:D_IN].reshape(BATCH, SEQ, KV_GQA, HEAD_DIM))
        new_ret.append(st)
        oa_l = _lat_na(z, cache_na_k, cache_na_v, _na_bias_tables(na_rpb[l]), l)
        oc_l = _lat_gqa(z, cache_gqa_k, cache_gqa_v, l)
        oret_l, _ = _retention(z, state_ret[:, l], log_g, NCTX, DEC_BATCH, DEC_SEQ)
        x = _out_proj(x, oa_l, oret_l, z, oc_l, w_out_b, g1, NCTX, NLAT)
        h, top_idx, gates = _router(x, norm2_g[l].reshape(1, D_MODEL), sc2, sh2, router_w[l],
                                    router_b[l].reshape(1, N_EXPERTS))
        plan, slot_src, slot_dst = _dispatch(top_idx[:, :TOP_K])
        bgu = b_gate_up[l]
        y = _experts(h, plan, slot_src, slot_dst, w_gate_up,
                     bgu[:, 0::2].reshape(N_EXPERTS, 1, D_FF), bgu[:, 1::2].reshape(N_EXPERTS, 1, D_FF),
                     w_down, b_down[l].reshape(N_EXPERTS, 1, D_MODEL), l)
        x = _combine(x, y, gates, g2)
    y_prompt = _final_norm(x, final_g.reshape(1, D_MODEL), 0, NCTX).reshape(BATCH, SEQ, D_MODEL)
    y_sample = _final_norm(x, final_g.reshape(1, D_MODEL), NCTX, NLAT).reshape(DEC_BATCH, DEC_SEQ, D_MODEL)
    return (y_prompt, y_sample, jnp.stack(new_na_k, axis=1), jnp.stack(new_na_v, axis=1),
            jnp.stack(new_gqa_k, axis=1), jnp.stack(new_gqa_v, axis=1), jnp.stack(new_ret, axis=1))
```

```python
import functools

import jax
import jax.numpy as jnp
import numpy as np
from jax import lax
from jax.experimental import pallas as pl
from jax.experimental.pallas import tpu as pltpu

F32 = jnp.float32
BF16 = jnp.bfloat16

D_MODEL = 1024
BATCH = 32
SEQ = 256
DEPTH = 2
DEC_BATCH = 4
DEC_SEQ = 4096
PAST_LEN = 256
GRID_W = 64
HEAD_DIM = 64
H_NA = 4
H_RET = 4
H_GQA = 8
KV_GQA = 2
D_NA = H_NA * HEAD_DIM
D_RET = H_RET * HEAD_DIM
D_GQA = H_GQA * HEAD_DIM
D_KV = KV_GQA * HEAD_DIM
D_IN = 3 * D_NA + 4 * D_RET + D_GQA + 2 * D_KV
WIN_R = 8
WIN_C = 16
RET_CHUNK = 128
N_EXPERTS = 32
TOP_K = 4
D_FF = D_MODEL
SWIGLU_LIMIT = 7.0
SWIGLU_ALPHA = 1.702
MOE_BLOCK = 256
ROPE_THETA = 10000.0
EPS = 1e-6
ATTN_SCALE = HEAD_DIM ** -0.5

NCTX = BATCH * SEQ
NLAT = DEC_BATCH * DEC_SEQ
NTOK = NCTX + NLAT
N_COND = 8
NK = NTOK * TOP_K
N_MOE_BLOCKS = NK // MOE_BLOCK + N_EXPERTS
LANES = 128
NEG_BIG = -1e30

C_AQ, C_AK, C_AV = 0, D_NA, 2 * D_NA
C_BQ = 3 * D_NA
C_BK = C_BQ + D_RET
C_BV = C_BK + D_RET
C_BG = C_BV + D_RET
C_CQ = C_BG + D_RET
C_CK = C_CQ + D_GQA
C_CV = C_CK + D_KV

ROW_TILE = 512
NA_QROWS = 4
NA_BAND = 12
GQA_TQ = 128
VMEM_LIMIT = 56 * 1024 * 1024


def _cparams(n_axes):
    return pltpu.CompilerParams(dimension_semantics=("arbitrary",) * n_axes,
                                vmem_limit_bytes=VMEM_LIMIT)


def _mod_index(i, tile):
    nctx = NCTX // tile
    per_batch = DEC_SEQ // tile
    return jnp.where(i < nctx, 0, 1 + (i - nctx) // per_batch)


def _dot(a, b):
    return jnp.dot(a, b, preferred_element_type=F32)


def _dot_nt(a, b):
    return lax.dot_general(a, b, (((1,), (1,)), ((), ())), preferred_element_type=F32)


def _dot_tn(a, b):
    return lax.dot_general(a, b, (((0,), (0,)), ((), ())), preferred_element_type=F32)


def _mod_kernel(c_ref, w_ref, b_ref, o_ref):
    c = c_ref[...]
    s = c * jax.nn.sigmoid(c)
    o_ref[...] = jnp.dot(s, w_ref[...], preferred_element_type=F32,
                         precision=lax.Precision.HIGHEST) + b_ref[...]


def _modulation(cond, ada_w, ada_b):
    tn = 1536
    return pl.pallas_call(
        _mod_kernel, name="modulation",
        grid=(DEPTH, 6 * D_MODEL // tn),
        in_specs=[pl.BlockSpec((N_COND, D_MODEL), lambda l, j: (0, 0)),
                  pl.BlockSpec((None, D_MODEL, tn), lambda l, j: (l, 0, j)),
                  pl.BlockSpec((None, 1, tn), lambda l, j: (l, 0, j))],
        out_specs=pl.BlockSpec((None, N_COND, tn), lambda l, j: (l, 0, j)),
        out_shape=jax.ShapeDtypeStruct((DEPTH, N_COND, 6 * D_MODEL), F32),
        compiler_params=_cparams(2),
    )(cond, ada_w, ada_b.reshape(DEPTH, 1, 6 * D_MODEL))


def _rms_mod(x, g, sc, sh):
    ms = jnp.mean(x * x, axis=-1, keepdims=True)
    return (x * lax.rsqrt(ms + EPS) * g) * (1.0 + sc) + sh


def _two_head_rsqrt(x):
    sq = x * x
    left = lax.broadcasted_iota(jnp.int32, x.shape, 1) < HEAD_DIM
    s_left = jnp.sum(jnp.where(left, sq, 0.0), axis=-1, keepdims=True)
    s_right = jnp.sum(jnp.where(left, 0.0, sq), axis=-1, keepdims=True)
    ms = jnp.where(left, s_left, s_right) * (1.0 / HEAD_DIM)
    return lax.rsqrt(ms + EPS)


def _rope128(x, cos, sin_signed):
    first = (lax.broadcasted_iota(jnp.int32, x.shape, 1) & (HEAD_DIM // 2)) == 0
    rot = jnp.where(first, pltpu.roll(x, LANES - HEAD_DIM // 2, 1), pltpu.roll(x, HEAD_DIM // 2, 1))
    return x * cos + rot * sin_signed


def _softmax_pv(scores, values):
    m = functools.reduce(jnp.maximum, [jnp.max(s, axis=-1, keepdims=True) for s in scores])
    ps = [jnp.exp(s - m) for s in scores]
    denom = functools.reduce(lambda a, b: a + b, [jnp.sum(p, axis=-1, keepdims=True) for p in ps])
    o = functools.reduce(lambda a, b: a + b, [_dot(p.astype(BF16), v) for p, v in zip(ps, values)])
    return o / denom


def _head(x, h):
    return x[:, h * HEAD_DIM:(h + 1) * HEAD_DIM]


TOK_SUB = D_MODEL // LANES


def _store_token_tiles(ref, row0, x):
    n = x.shape[0]
    for s in range(TOK_SUB):
        ref[pl.ds(row0 + s, n, stride=TOK_SUB), :] = x[:, s * LANES:(s + 1) * LANES]


def _load_token_chunk(ref, row0, n, s):
    return ref[pl.ds(row0 + s, n, stride=TOK_SUB), :]


def _in_proj_kernel(x_ref, g_ref, sc_ref, sh_ref, w_ref, rc_ref, rs_ref, ac_ref, as_ref,
                    qg_ref, kg_ref, z_ref):
    hb = _rms_mod(x_ref[...], g_ref[...], sc_ref[...], sh_ref[...]).astype(BF16)

    def proj(c0, c1):
        return _dot(hb, w_ref[:, c0:c1])

    z_ref[:, C_AQ:C_AK] = proj(C_AQ, C_AK) * ATTN_SCALE
    z_ref[:, C_AK:C_BQ] = proj(C_AK, C_BQ)
    rc, rs = rc_ref[...], rs_ref[...]
    zq = proj(C_BQ, C_BK)
    zk = proj(C_BK, C_BV)
    for j in range(D_RET // LANES):
        sl = slice(j * LANES, (j + 1) * LANES)
        z_ref[:, C_BQ + j * LANES:C_BQ + (j + 1) * LANES] = _rope128(zq[:, sl], rc, rs)
        z_ref[:, C_BK + j * LANES:C_BK + (j + 1) * LANES] = _rope128(zk[:, sl], rc, rs) * ATTN_SCALE
    z_ref[:, C_BV:C_CQ] = proj(C_BV, C_CQ)
    ac, asn = ac_ref[...], as_ref[...]
    zc = proj(C_CQ, C_CK)
    for j in range(D_GQA // LANES):
        t = zc[:, j * LANES:(j + 1) * LANES]
        t = t * _two_head_rsqrt(t) * qg_ref[...]
        z_ref[:, C_CQ + j * LANES:C_CQ + (j + 1) * LANES] = _rope128(t, ac, asn) * ATTN_SCALE
    zkv = proj(C_CK, D_IN)
    t = zkv[:, :LANES]
    t = t * _two_head_rsqrt(t) * kg_ref[...]
    z_ref[:, C_CK:C_CV] = _rope128(t, ac, asn)
    z_ref[:, C_CV:D_IN] = zkv[:, LANES:]


def _in_proj(x, g, sc, sh, w_b, tabs, qg, kg):
    tm = ROW_TILE
    nctx = NCTX // tm
    per_seq = DEC_SEQ // tm

    def tab_idx(i):
        return (jnp.where(i < nctx, per_seq, (i - nctx) % per_seq), 0)

    row = lambda i: (i, 0)
    fixed = lambda i: (0, 0)
    mod = lambda i: (_mod_index(i, tm), 0, 0)
    tab_spec = pl.BlockSpec((tm, LANES), tab_idx)
    return pl.pallas_call(
        _in_proj_kernel, name="in_proj",
        grid=(NTOK // tm,),
        in_specs=[pl.BlockSpec((tm, D_MODEL), row),
                  pl.BlockSpec((1, D_MODEL), fixed),
                  pl.BlockSpec((None, 1, D_MODEL), mod),
                  pl.BlockSpec((None, 1, D_MODEL), mod),
                  pl.BlockSpec((D_MODEL, D_IN), fixed),
                  tab_spec, tab_spec, tab_spec, tab_spec,
                  pl.BlockSpec((1, LANES), fixed),
                  pl.BlockSpec((1, LANES), fixed)],
        out_specs=pl.BlockSpec((tm, D_IN), row),
        out_shape=jax.ShapeDtypeStruct((NTOK, D_IN), F32),
        compiler_params=_cparams(1),
    )(x, g, sc, sh, w_b, *tabs, qg, kg)


def _rope_tables():
    t = np.arange(DEC_SEQ)
    inv_ret = 1.0 / (ROPE_THETA ** np.linspace(0.0, 1.0, HEAD_DIM // 2, dtype=np.float32))
    ang_ret = t.astype(np.float32)[:, None] * inv_ret.astype(np.float32)
    n = HEAD_DIM // 4
    inv_ax = (ROPE_THETA ** (-np.arange(n, dtype=np.float32) / n)).astype(np.float32)
    row = (t // GRID_W).astype(np.float32)
    col = (t % GRID_W).astype(np.float32)
    ang_ax = np.concatenate([row[:, None] * inv_ax, col[:, None] * inv_ax], axis=-1)

    def tables(ang):
        ang = jnp.asarray(ang, F32)
        cos, sin = jnp.cos(ang), jnp.sin(ang)
        cos2 = jnp.tile(jnp.concatenate([cos, cos], axis=-1), (1, LANES // HEAD_DIM))
        sin2 = jnp.tile(jnp.concatenate([-sin, sin], axis=-1), (1, LANES // HEAD_DIM))
        cos2 = jnp.concatenate([cos2, jnp.ones((ROW_TILE, LANES), F32)], axis=0)
        sin2 = jnp.concatenate([sin2, jnp.zeros((ROW_TILE, LANES), F32)], axis=0)
        return cos2, sin2

    return (*tables(ang_ret), *tables(ang_ax))


def _ctx_attn_kernel(aq_ref, ak_ref, av_ref, cq0_ref, cq1_ref, ck_ref, cv_ref, oa_ref, oc_ref):
    aq = aq_ref[...].astype(BF16)
    ak = ak_ref[...].astype(BF16)
    av = av_ref[...].astype(BF16)
    outs = []
    for h in range(H_NA):
        s = _dot_nt(_head(aq, h), _head(ak, h))
        outs.append(_softmax_pv([s], [_head(av, h)]))
    oa_ref[...] = jnp.concatenate(outs, axis=-1)
    ck = ck_ref[...].astype(BF16)
    cv = cv_ref[...].astype(BF16)
    group = H_GQA // KV_GQA
    outs = []
    for g, cq_ref in enumerate((cq0_ref, cq1_ref)):
        cq = cq_ref[...].astype(BF16)
        qs = jnp.concatenate([_head(cq, j) for j in range(group)], axis=0)
        o = _softmax_pv([_dot_nt(qs, _head(ck, g))], [_head(cv, g)])
        outs += [o[j * SEQ:(j + 1) * SEQ] for j in range(group)]
    oc_ref[...] = jnp.concatenate(outs, axis=-1)


def _ctx_attention(z):
    col = lambda c, w: pl.BlockSpec((SEQ, w), lambda b: (b, c // w))
    return pl.pallas_call(
        _ctx_attn_kernel, name="ctx_attn",
        grid=(BATCH,),
        in_specs=[col(C_AQ, D_NA), col(C_AK, D_NA), col(C_AV, D_NA),
                  col(C_CQ, D_NA), col(C_CQ + D_NA, D_NA), col(C_CK, D_KV), col(C_CV, D_KV)],
        out_specs=[pl.BlockSpec((SEQ, D_NA), lambda b: (b, 0)),
                   pl.BlockSpec((SEQ, D_GQA), lambda b: (b, 0))],
        out_shape=[jax.ShapeDtypeStruct((NCTX, D_NA), F32),
                   jax.ShapeDtypeStruct((NCTX, D_GQA), F32)],
        compiler_params=_cparams(1),
    )(z, z, z, z, z, z, z)


def _lat_gqa_kernel(cq0_ref, cq1_ref, ck_ref, cv_ref, kc_ref, vc_ref, oc_ref):
    ck = ck_ref[...].astype(BF16)
    cv = cv_ref[...].astype(BF16)
    kc = kc_ref[...].astype(BF16)
    vc = vc_ref[...].astype(BF16)
    group = H_GQA // KV_GQA
    outs = []
    for g, cq_ref in enumerate((cq0_ref, cq1_ref)):
        cq = cq_ref[...].astype(BF16)
        qs = jnp.concatenate([_head(cq, j) for j in range(group)], axis=0)
        o = _softmax_pv([_dot_nt(qs, _head(kc, g)), _dot_nt(qs, _head(ck, g))],
                        [_head(vc, g), _head(cv, g)])
        outs += [o[j * GQA_TQ:(j + 1) * GQA_TQ] for j in range(group)]
    oc_ref[...] = jnp.concatenate(outs, axis=-1)


def _lat_gqa(z, cache_k, cache_v, layer):
    nq = DEC_SEQ // GQA_TQ
    q0 = NCTX // GQA_TQ
    s0 = NCTX // DEC_SEQ
    qspec = lambda c: pl.BlockSpec((GQA_TQ, D_NA), lambda b, i: (q0 + b * nq + i, c // D_NA))
    kvspec = lambda c: pl.BlockSpec((DEC_SEQ, D_KV), lambda b, i: (s0 + b, c // D_KV))
    cspec = pl.BlockSpec((None, None, PAST_LEN, D_KV), lambda b, i: (b, layer, 0, 0))
    return pl.pallas_call(
        _lat_gqa_kernel, name="lat_gqa",
        grid=(DEC_BATCH, nq),
        in_specs=[qspec(C_CQ), qspec(C_CQ + D_NA), kvspec(C_CK), kvspec(C_CV), cspec, cspec],
        out_specs=pl.BlockSpec((GQA_TQ, D_GQA), lambda b, i: (b * nq + i, 0)),
        out_shape=jax.ShapeDtypeStruct((NLAT, D_GQA), F32),
        compiler_params=_cparams(2),
    )(z, z, z, z, cache_k, cache_v)


def _na_band_start(i):
    return jnp.clip(NA_QROWS * i - WIN_R // 2, 0, GRID_W - NA_BAND)


def _lat_na_kernel(q_ref, k_ref, v_ref, kc_ref, vc_ref, bias_ref, o_ref):
    i = pl.program_id(1)
    start = pl.multiple_of(_na_band_start(i) * GRID_W, GRID_W)
    q = q_ref[...].astype(BF16)
    kb = k_ref[pl.ds(start, NA_BAND * GRID_W), :].astype(BF16)
    vb = v_ref[pl.ds(start, NA_BAND * GRID_W), :].astype(BF16)
    kc = kc_ref[...].astype(BF16)
    vc = vc_ref[...].astype(BF16)
    outs = []
    for h in range(H_NA):
        qh = _head(q, h)
        s_loc = _dot_nt(qh, _head(kb, h)) + bias_ref[h]
        s_ctx = _dot_nt(qh, _head(kc, h))
        outs.append(_softmax_pv([s_loc, s_ctx], [_head(vb, h), _head(vc, h)]))
    o_ref[...] = jnp.concatenate(outs, axis=-1)


def _na_bias_tables(rpb):
    nq, nk = NA_QROWS * GRID_W, NA_BAND * GRID_W
    rows = DEC_SEQ // GRID_W
    nblk = rows // NA_QROWS
    n_rb, n_cb = 2 * WIN_R - 1, 2 * WIN_C - 1
    row_sel = np.zeros((3, NA_QROWS, NA_BAND, n_rb), np.float32)
    row_ok = np.zeros((3, NA_QROWS, NA_BAND), bool)
    for kind, blk in enumerate((0, 1, nblk - 1)):
        bs = int(np.clip(NA_QROWS * blk - WIN_R // 2, 0, rows - NA_BAND))
        for jr in range(NA_QROWS):
            r = NA_QROWS * blk + jr
            rs = int(np.clip(r - WIN_R // 2, 0, rows - WIN_R))
            for bi in range(NA_BAND):
                ri = bs + bi
                if rs <= ri < rs + WIN_R:
                    row_ok[kind, jr, bi] = True
                    row_sel[kind, jr, bi, ri - r + WIN_R - 1] = 1.0
    c = np.arange(GRID_W)
    cs = np.clip(c - WIN_C // 2, 0, GRID_W - WIN_C)
    col_ok = (c[None, :] >= cs[:, None]) & (c[None, :] < cs[:, None] + WIN_C)
    col_sel = np.zeros((n_cb, GRID_W, GRID_W), np.float32)
    qc, kc = np.nonzero(col_ok)
    col_sel[kc - qc + WIN_C - 1, qc, kc] = 1.0
    hi = lax.Precision.HIGHEST
    by_row = jnp.einsum('kjba,hax->hkjbx', row_sel, rpb, precision=hi)
    bias = jnp.einsum('hkjbx,xcz->khjcbz', by_row, col_sel, precision=hi)
    ok = row_ok[:, None, :, None, :, None] & col_ok[None, None, None, :, None, :]
    return jnp.where(ok, bias, NEG_BIG).reshape(3, H_NA, nq, nk).astype(F32)


def _lat_na(z, cache_k, cache_v, bias, layer):
    nq_rows = NA_QROWS * GRID_W
    nblk = DEC_SEQ // nq_rows
    q0 = NCTX // nq_rows
    s0 = NCTX // DEC_SEQ
    kvspec = lambda c: pl.BlockSpec((DEC_SEQ, D_NA), lambda b, i: (s0 + b, c // D_NA))
    cspec = pl.BlockSpec((None, None, PAST_LEN, D_NA), lambda b, i: (b, layer, 0, 0))
    kind = lambda b, i: (jnp.where(i == 0, 0, jnp.where(i == nblk - 1, 2, 1)), 0, 0, 0)
    return pl.pallas_call(
        _lat_na_kernel, name="lat_na",
        grid=(DEC_BATCH, nblk),
        in_specs=[pl.BlockSpec((nq_rows, D_NA), lambda b, i: (q0 + b * nblk + i, 0)),
                  kvspec(C_AK), kvspec(C_AV), cspec, cspec,
                  pl.BlockSpec((None, H_NA, nq_rows, NA_BAND * GRID_W), kind)],
        out_specs=pl.BlockSpec((nq_rows, D_NA), lambda b, i: (b * nblk + i, 0)),
        out_shape=jax.ShapeDtypeStruct((NLAT, D_NA), F32),
        compiler_params=_cparams(2),
    )(z, z, z, cache_k, cache_v, bias)


def _ret_kernel(lg_ref, q_ref, k_ref, v_ref, s0_ref, lgl_ref, o_ref, sfin_ref, s_scr, *, n_chunks):
    d = pl.program_id(1)
    C = RET_CHUNK
    s_scr[...] = s0_ref[...].astype(F32)
    fwd = d == 0
    row = lax.broadcasted_iota(jnp.int32, (C, 1), 0).astype(F32)
    pos_q = jnp.where(fwd, row + 1.0, C - row)
    pos_k = jnp.where(fwd, C - 1.0 - row, row)
    lgl = lgl_ref[...]
    q_decay = jnp.exp(lgl * pos_q)
    k_decay = jnp.exp(lgl * pos_k)
    chunk_decay = jnp.exp(lgl * float(C))
    ii = lax.broadcasted_iota(jnp.int32, (C, C), 0)
    jj = lax.broadcasted_iota(jnp.int32, (C, C), 1)
    delta = jnp.where(fwd, ii - jj, jj - ii).astype(F32)
    masks = [jnp.where(delta >= 0.0, jnp.exp(lg_ref[d, h] * jnp.maximum(delta, 0.0)), 0.0)
             for h in range(H_RET)]

    def step(c, carry):
        r0 = pl.multiple_of(jnp.where(fwd, c, n_chunks - 1 - c) * C, C)
        q = q_ref[pl.ds(r0, C), :]
        k = k_ref[pl.ds(r0, C), :]
        vb = v_ref[pl.ds(r0, C), :].astype(BF16)
        kd = (k * k_decay).astype(BF16)
        qb = q.astype(BF16)
        kb = k.astype(BF16)
        outs = []
        for h in range(H_RET):
            qh, kh, vh = _head(qb, h), _head(kb, h), _head(vb, h)
            inner = _dot((_dot_nt(qh, kh) * masks[h]).astype(BF16), vh)
            s = s_scr[h]
            cross = _dot(qh, s.astype(BF16)) * _head(q_decay, h)
            s_scr[h] = s * _head(chunk_decay, h) + _dot_tn(_head(kd, h), vh)
            outs.append(inner + cross)
        o_ref[pl.ds(r0, C), :] = jnp.concatenate(outs, axis=-1)
        return carry

    lax.fori_loop(0, n_chunks, step, 0)
    sfin_ref[...] = s_scr[...]


def _retention(z, s0, log_g, row0, n_batch, seq):
    n = seq // RET_CHUNK
    b0 = row0 // seq
    lgl = jnp.repeat(log_g, HEAD_DIM, axis=-1).reshape(2, 1, D_RET)
    zspec = lambda col: pl.BlockSpec((seq, D_RET), lambda b, d, lg: (b0 + b, col // D_RET))
    sspec = pl.BlockSpec((None, None, H_RET, HEAD_DIM, HEAD_DIM), lambda b, d, lg: (b, d, 0, 0, 0))
    return pl.pallas_call(
        functools.partial(_ret_kernel, n_chunks=n), name="retention",
        grid_spec=pltpu.PrefetchScalarGridSpec(
            num_scalar_prefetch=1,
            grid=(n_batch, 2),
            in_specs=[zspec(C_BQ), zspec(C_BK), zspec(C_BV), sspec,
                      pl.BlockSpec((None, 1, D_RET), lambda b, d, lg: (d, 0, 0))],
            out_specs=[pl.BlockSpec((None, seq, D_RET), lambda b, d, lg: (d, b, 0)),
                       sspec],
            scratch_shapes=[pltpu.VMEM((H_RET, HEAD_DIM, HEAD_DIM), F32)]),
        out_shape=[jax.ShapeDtypeStruct((2, n_batch * seq, D_RET), F32),
                   jax.ShapeDtypeStruct((n_batch, 2, H_RET, HEAD_DIM, HEAD_DIM), F32)],
        compiler_params=_cparams(2),
    )(log_g, z, z, z, s0, lgl)


def _out_proj_kernel(x_ref, oa_ref, of_ref, ob_ref, bg_ref, oc_ref, w_ref, gate_ref, y_ref):
    ob = of_ref[...] + ob_ref[...]
    bg = bg_ref[...]
    parts = []
    for j in range(D_RET // LANES):
        t = ob[:, j * LANES:(j + 1) * LANES]
        g = bg[:, j * LANES:(j + 1) * LANES]
        parts.append(t * _two_head_rsqrt(t) * (g * jax.nn.sigmoid(g)))
    obn = jnp.concatenate(parts, axis=-1).astype(BF16)
    o = (_dot(oa_ref[...].astype(BF16), w_ref[0:D_NA, :])
         + _dot(obn, w_ref[D_NA:D_NA + D_RET, :])
         + _dot(oc_ref[...].astype(BF16), w_ref[D_NA + D_RET:, :]))
    y_ref[...] = x_ref[...] + gate_ref[...] * o


def _out_proj(x, oa, o_ret, z, oc, w_b, gate, row0, rows):
    tm = ROW_TILE
    r0 = row0 // tm
    loc = lambda i: (i, 0)
    return pl.pallas_call(
        _out_proj_kernel, name="out_proj",
        grid=(rows // tm,),
        in_specs=[pl.BlockSpec((tm, D_MODEL), lambda i: (r0 + i, 0)),
                  pl.BlockSpec((tm, D_NA), loc),
                  pl.BlockSpec((None, tm, D_RET), lambda i: (0, i, 0)),
                  pl.BlockSpec((None, tm, D_RET), lambda i: (1, i, 0)),
                  pl.BlockSpec((tm, D_RET), lambda i: (r0 + i, C_BG // D_RET)),
                  pl.BlockSpec((tm, D_GQA), loc),
                  pl.BlockSpec((D_MODEL, D_MODEL), lambda i: (0, 0)),
                  pl.BlockSpec((None, 1, D_MODEL), lambda i: (_mod_index(r0 + i, tm), 0, 0))],
        out_specs=pl.BlockSpec((tm, D_MODEL), lambda i: (r0 + i, 0)),
        out_shape=jax.ShapeDtypeStruct((NTOK, D_MODEL), F32),
        input_output_aliases={0: 0},
        compiler_params=_cparams(1),
    )(x, oa, o_ret, o_ret, z, oc, w_b, gate)


def _router_kernel(x_ref, g_ref, sc_ref, sh_ref, w_ref, b_ref, h_ref, idx_ref, gate_ref):
    h = _rms_mod(x_ref[...], g_ref[...], sc_ref[...], sh_ref[...])
    _store_token_tiles(h_ref, 0, h)
    w = w_ref[...]
    w_hi = w.astype(BF16)
    w_lo = (w - w_hi.astype(F32)).astype(BF16)
    h_hi = h.astype(BF16)
    h_lo = (h - h_hi.astype(F32)).astype(BF16)
    logits = _dot(h_hi, w_hi) + (_dot(h_hi, w_lo) + _dot(h_lo, w_hi)) + b_ref[...]
    lane = lax.broadcasted_iota(jnp.int32, logits.shape, 1)
    out_lane = lax.broadcasted_iota(jnp.int32, idx_ref.shape, 1)
    idx_out = jnp.zeros(idx_ref.shape, jnp.int32)
    val_out = jnp.zeros(gate_ref.shape, F32)
    top = None
    denom = None
    for k in range(TOP_K):
        m = jnp.max(logits, axis=-1, keepdims=True)
        i = jnp.min(jnp.where(logits == m, lane, N_EXPERTS), axis=-1, keepdims=True)
        logits = jnp.where(lane == i, -jnp.inf, logits)
        if k == 0:
            top = m
        e = jnp.exp(m - top)
        denom = e if k == 0 else denom + e
        idx_out = jnp.where(out_lane == k, i, idx_out)
        val_out = jnp.where(out_lane == k, e, val_out)
    idx_ref[...] = idx_out
    gate_ref[...] = val_out / denom


def _router(x, g, sc, sh, rw, rb):
    tm = ROW_TILE
    row = lambda i: (i, 0)
    fixed = lambda i: (0, 0)
    mod = lambda i: (_mod_index(i, tm), 0, 0)
    return pl.pallas_call(
        _router_kernel, name="router",
        grid=(NTOK // tm,),
        in_specs=[pl.BlockSpec((tm, D_MODEL), row),
                  pl.BlockSpec((1, D_MODEL), fixed),
                  pl.BlockSpec((None, 1, D_MODEL), mod),
                  pl.BlockSpec((None, 1, D_MODEL), mod),
                  pl.BlockSpec((D_MODEL, N_EXPERTS), fixed),
                  pl.BlockSpec((1, N_EXPERTS), fixed)],
        out_specs=[pl.BlockSpec((tm * TOK_SUB, LANES), row),
                   pl.BlockSpec((tm, LANES), row),
                   pl.BlockSpec((tm, LANES), row)],
        out_shape=[jax.ShapeDtypeStruct((NTOK * TOK_SUB, LANES), F32),
                   jax.ShapeDtypeStruct((NTOK, LANES), jnp.int32),
                   jax.ShapeDtypeStruct((NTOK, LANES), F32)],
        compiler_params=_cparams(1),
    )(x, g, sc, sh, rw, rb)


PAD_ROW0 = TOP_K * NTOK
Y_ROWS = PAD_ROW0 + 2 * MOE_BLOCK
GU_CHUNK = 2 * LANES


def _expert_kernel(be_ref, nv_ref, ne_ref, ws_ref, src0_ref, srcn_ref, dst_ref, h_hbm, wgu_hbm, bg_ref,
                   bl_ref, wd_hbm, bd_ref, y_hbm, xbuf, ybuf, wg_s, wl_s, wd_s, t_scr, wgu_buf, wd_buf,
                   gsem, ssem, wsem, *, layer):
    j = pl.program_id(0)
    nv = nv_ref[0]
    cur = j % 2
    nxt = 1 - cur
    half = MOE_BLOCK * TOK_SUB

    def tile(ref, t):
        start = t * TOK_SUB
        if not isinstance(start, int):
            start = pl.multiple_of(start, TOK_SUB)
        return ref.at[pl.ds(start, TOK_SUB), :]

    def issue_gather(src_ref, slot):
        for r in range(MOE_BLOCK):
            pltpu.make_async_copy(tile(h_hbm, src_ref[0, r]), tile(xbuf, slot * MOE_BLOCK + r),
                                  gsem.at[slot]).start()

    def issue_scatter(slot, dst_of):
        for r in range(MOE_BLOCK):
            pltpu.make_async_copy(tile(ybuf, slot * MOE_BLOCK + r), tile(y_hbm, dst_of(r)),
                                  ssem.at[slot]).start()

    def wait_half(buf, sem, slot):
        whole = buf.at[pl.ds(pl.multiple_of(slot * half, half), half), :]
        pltpu.make_async_copy(whole, whole, sem.at[slot]).wait()

    wait_gather = functools.partial(wait_half, xbuf, gsem)
    wait_scatter = functools.partial(wait_half, ybuf, ssem)

    def weight_copies(e, p):
        return (pltpu.make_async_copy(wgu_hbm.at[layer, e], wgu_buf.at[p], wsem.at[p, 0]),
                pltpu.make_async_copy(wd_hbm.at[layer, e], wd_buf.at[p], wsem.at[p, 1]))

    @pl.when(j == 0)
    def _():
        issue_gather(src0_ref, 0)
        ybuf[...] = jnp.zeros(ybuf.shape, F32)
        for slot in range(2):
            issue_scatter(slot, lambda r, slot=slot: PAD_ROW0 + slot * MOE_BLOCK + r)
        for cp in weight_copies(be_ref[0], 0):
            cp.start()

    changed =jnp.logical_or(j == 0, be_ref[j] != be_ref[jnp.maximum(j - 1, 0)])

    @pl.when(jnp.logical_and(j < nv, changed))
    def _():
        p = ws_ref[j]
        for cp in weight_copies(be_ref[j], p):
            cp.wait()

        @pl.when(ne_ref[j] >= 0)
        def _():
            for cp in weight_copies(ne_ref[j], 1 - p):
                cp.start()

        wgu_ref = wgu_buf.at[p]
        wd_ref = wd_buf.at[p]
        for c in range(2 * D_FF // GU_CHUNK):
            t = wgu_ref[:, c * GU_CHUNK:(c + 1) * GU_CHUNK].T
            for kk in range(D_MODEL // LANES):
                t_scr[kk] = t[:, kk * LANES:(kk + 1) * LANES]
            rows = slice(c * GU_CHUNK // 2, (c + 1) * GU_CHUNK // 2)
            for kk in range(D_MODEL // LANES):
                cols = slice(kk * LANES, (kk + 1) * LANES)
                wg_s[rows, cols] = t_scr[kk, pl.ds(0, GU_CHUNK // 2, stride=2), :].astype(BF16)
                wl_s[rows, cols] = t_scr[kk, pl.ds(1, GU_CHUNK // 2, stride=2), :].astype(BF16)
        wd_s[...] = wd_ref[...].astype(BF16)

    @pl.when(j < nv)
    def _():
        wait_scatter(cur)
        wait_gather(cur)
        issue_gather(srcn_ref, nxt)
        base = pl.multiple_of(cur * half, half)
        xb = jnp.concatenate([_load_token_chunk(xbuf, base, MOE_BLOCK, s).astype(BF16)
                              for s in range(TOK_SUB)], axis=-1)
        glu = jnp.minimum(_dot_nt(xb, wg_s[...]) + bg_ref[...], SWIGLU_LIMIT)
        lin = jnp.clip(_dot_nt(xb, wl_s[...]) + bl_ref[...], -SWIGLU_LIMIT, SWIGLU_LIMIT)
        act = glu * jax.nn.sigmoid(SWIGLU_ALPHA * glu) * (lin + 1.0)
        _store_token_tiles(ybuf, base, _dot(act.astype(BF16), wd_s[...]) + bd_ref[...])
        issue_scatter(cur, lambda r: dst_ref[0, r])

    @pl.when(j == nv - 1)
    def _():
        wait_gather(nxt)
        wait_scatter(nxt)
        wait_scatter(cur)


def _experts(h, plan, slot_src, slot_dst, wgu, bg, bl, wd, bd, layer):
    wmap = lambda j, be, *_: (be[j], 0, 0)
    idx_spec = lambda fn: pl.BlockSpec((None, 1, MOE_BLOCK), fn, memory_space=pltpu.SMEM)
    return pl.pallas_call(
        functools.partial(_expert_kernel, layer=layer),
        grid_spec=pltpu.PrefetchScalarGridSpec(
            num_scalar_prefetch=len(plan),
            grid=(N_MOE_BLOCKS,),
            in_specs=[idx_spec(lambda j, *_: (0, 0, 0)),
                      idx_spec(lambda j, *_: (jnp.minimum(j + 1, N_MOE_BLOCKS - 1), 0, 0)),
                      idx_spec(lambda j, *_: (j, 0, 0)),
                      pl.BlockSpec(memory_space=pl.ANY),
                      pl.BlockSpec(memory_space=pl.ANY),
                      pl.BlockSpec((None, 1, D_FF), wmap),
                      pl.BlockSpec((None, 1, D_FF), wmap),
                      pl.BlockSpec(memory_space=pl.ANY),
                      pl.BlockSpec((None, 1, D_MODEL), wmap)],
            out_specs=pl.BlockSpec(memory_space=pl.ANY),
            scratch_shapes=[pltpu.VMEM((2 * MOE_BLOCK * TOK_SUB, LANES), F32),
                            pltpu.VMEM((2 * MOE_BLOCK * TOK_SUB, LANES), F32),
                            pltpu.VMEM((D_FF, D_MODEL), BF16),
                            pltpu.VMEM((D_FF, D_MODEL), BF16),
                            pltpu.VMEM((D_FF, D_MODEL), BF16),
                            pltpu.VMEM((D_MODEL // LANES, GU_CHUNK, LANES), F32),
                            pltpu.VMEM((2, D_MODEL, 2 * D_FF), F32),
                            pltpu.VMEM((2, D_FF, D_MODEL), F32),
                            pltpu.SemaphoreType.DMA((2,)),
                            pltpu.SemaphoreType.DMA((2,)),
                            pltpu.SemaphoreType.DMA((2, 2))]),
        out_shape=jax.ShapeDtypeStruct((Y_ROWS * TOK_SUB, LANES), F32),
        compiler_params=_cparams(1),
        name="experts",
    )(*plan, slot_src, slot_src, slot_dst, h, wgu, bg, bl, wd, bd)


def _dispatch(top_idx):
    flat_e = top_idx.reshape(-1)
    order = jnp.argsort(flat_e, stable=True).astype(jnp.int32)
    experts = jnp.arange(N_EXPERTS, dtype=jnp.int32)
    counts = jnp.sum((flat_e[:, None] == experts[None, :]).astype(jnp.int32), axis=0)
    padded = ((counts + MOE_BLOCK - 1) // MOE_BLOCK) * MOE_BLOCK
    pad_end = jnp.cumsum(padded)
    pad_start = pad_end - padded
    start = jnp.cumsum(counts) - counts
    block0 = jnp.arange(N_MOE_BLOCKS, dtype=jnp.int32) * MOE_BLOCK
    n_valid = (pad_end[-1] // MOE_BLOCK).astype(jnp.int32).reshape(1)
    used = block0 < pad_end[-1]
    block_e = jnp.sum((pad_end[None, :] <= block0[:, None]).astype(jnp.int32), axis=1)
    block_e = jnp.minimum(block_e, N_EXPERTS - 1)
    onehot = block_e[:, None] == experts[None, :]
    pick = lambda v: jnp.sum(jnp.where(onehot, v[None, :], 0), axis=1)
    offset = block0 - pick(pad_start)
    first = pick(start) + offset
    left = jnp.where(used, pick(counts) - offset, 0)
    r = jnp.arange(MOE_BLOCK, dtype=jnp.int32)[None, :]
    real = r < left[:, None]
    pos = jnp.clip(first[:, None] + r, 0, NK - 1)
    aid = jnp.take(order, pos, axis=0)
    pad_dst = PAD_ROW0 + ((block0 // MOE_BLOCK) % 2)[:, None] * MOE_BLOCK + r
    slot_src = jnp.where(real, aid // TOP_K, 0)
    slot_dst = jnp.where(real, (aid % TOP_K) * NTOK + aid // TOP_K, pad_dst)
    run_first = jnp.logical_and(used, offset == 0)
    w_half = ((jnp.cumsum(run_first.astype(jnp.int32)) - 1) % 2).astype(jnp.int32)
    next_block = pick(pad_end) // MOE_BLOCK
    next_e = jnp.take(block_e, jnp.minimum(next_block, N_MOE_BLOCKS - 1), axis=0)
    next_e = jnp.where(jnp.logical_and(used, next_block < n_valid[0]), next_e, -1).astype(jnp.int32)
    last_e = jnp.sum(jnp.where(block0 == (n_valid[0] - 1) * MOE_BLOCK, block_e, 0))
    block_e = jnp.where(used, block_e, last_e).astype(jnp.int32)
    shape = (N_MOE_BLOCKS, 1, MOE_BLOCK)
    return (block_e, n_valid, next_e, w_half), slot_src.reshape(shape), slot_dst.reshape(shape)


def _combine_kernel(x_ref, *refs):
    y_refs, (gate_ref, mod_ref, o_ref) = refs[:TOP_K], refs[TOP_K:]
    gates = gate_ref[...]
    n = x_ref.shape[0]
    for s in range(TOK_SUB):
        cols = slice(s * LANES, (s + 1) * LANES)
        acc = gates[:, 0:1] * _load_token_chunk(y_refs[0], 0, n, s)
        for k in range(1, TOP_K):
            acc = acc + gates[:, k:k + 1] * _load_token_chunk(y_refs[k], 0, n, s)
        o_ref[:, cols] = x_ref[:, cols] + mod_ref[:, cols] * acc


def _combine(x, y, gates, gate2):
    tm = ROW_TILE
    row = lambda i: (i, 0)
    plane = lambda k: pl.BlockSpec((tm * TOK_SUB, LANES), lambda i: (k * (NTOK // tm) + i, 0))
    return pl.pallas_call(
        _combine_kernel,
        grid=(NTOK // tm,),
        in_specs=[pl.BlockSpec((tm, D_MODEL), row)] + [plane(k) for k in range(TOP_K)]
        + [pl.BlockSpec((tm, LANES), row),
           pl.BlockSpec((None, 1, D_MODEL), lambda i: (_mod_index(i, tm), 0, 0))],
        out_specs=pl.BlockSpec((tm, D_MODEL), row),
        out_shape=jax.ShapeDtypeStruct((NTOK, D_MODEL), F32),
        input_output_aliases={0: 0},
        compiler_params=_cparams(1),
        name="combine",
    )(x, *([y] * TOP_K), gates, gate2)


def _final_norm_kernel(x_ref, g_ref, o_ref):
    x = x_ref[...]
    ms = jnp.mean(x * x, axis=-1, keepdims=True)
    o_ref[...] = x * lax.rsqrt(ms + EPS) * g_ref[...]


def _final_norm(x, g, row0, rows):
    tm = ROW_TILE
    r0 = row0 // tm
    return pl.pallas_call(
        _final_norm_kernel, name="final_norm",
        grid=(rows // tm,),
        in_specs=[pl.BlockSpec((tm, D_MODEL), lambda i: (r0 + i, 0)),
                  pl.BlockSpec((1, D_MODEL), lambda i: (0, 0))],
        out_specs=pl.BlockSpec((tm, D_MODEL), lambda i: (i, 0)),
        out_shape=jax.ShapeDtypeStruct((rows, D_MODEL), F32),
        compiler_params=_cparams(1),
    )(x, g)


def kernel(x_prompt, x_sample, cache_na_k, cache_na_v, cache_gqa_k, cache_gqa_v, state_ret, c, c_ctx,
           ada_w, ada_b, norm1_g, norm2_g, w_in, w_out, na_rpb, ret_decay_logit, q_norm_g, k_norm_g,
           router_w, router_b, w_gate_up, b_gate_up, w_down, b_down, final_g):
    x = jnp.concatenate([x_prompt.reshape(NCTX, D_MODEL), x_sample.reshape(NLAT, D_MODEL)], axis=0)
    cond = jnp.zeros((N_COND, D_MODEL), F32).at[0].set(c_ctx).at[1:1 + DEC_BATCH].set(c)
    mod = _modulation(cond, ada_w, ada_b)
    tabs = _rope_tables()
    cache_na_k = cache_na_k.reshape(DEC_BATCH, DEPTH, PAST_LEN, D_NA)
    cache_na_v = cache_na_v.reshape(DEC_BATCH, DEPTH, PAST_LEN, D_NA)
    cache_gqa_k = cache_gqa_k.reshape(DEC_BATCH, DEPTH, PAST_LEN, D_KV)
    cache_gqa_v = cache_gqa_v.reshape(DEC_BATCH, DEPTH, PAST_LEN, D_KV)
    zero_state = jnp.zeros((BATCH, 2, H_RET, HEAD_DIM, HEAD_DIM), F32)
    two_heads = lambda g: jnp.tile(g, LANES // HEAD_DIM).reshape(1, LANES)
    new_na_k, new_na_v, new_gqa_k, new_gqa_v, new_ret = [], [], [], [], []
    for l in range(DEPTH):
        sh1, sc1, g1, sh2, sc2, g2 = [m.reshape(N_COND, 1, D_MODEL) for m in jnp.split(mod[l], 6, axis=-1)]
        z = _in_proj(x, norm1_g[l].reshape(1, D_MODEL), sc1, sh1, w_in[l].astype(BF16), tabs,
                     two_heads(q_norm_g[l]), two_heads(k_norm_g[l]))
        log_g = jax.nn.log_sigmoid(ret_decay_logit[l].astype(F32))
        w_out_b = w_out[l].astype(BF16)
        oa_c, oc_c = _ctx_attention(z)
        oret_c, st = _retention(z, zero_state, log_g, 0, BATCH, SEQ)
        x = _out_proj(x, oa_c, oret_c, z, oc_c, w_out_b, g1, 0, NCTX)
        zc = z[:NCTX]
        new_na_k.append(zc[:, C_AK:C_AV].reshape(BATCH, SEQ, H_NA, HEAD_DIM))
        new_na_v.append(zc[:, C_AV:C_BQ].reshape(BATCH, SEQ, H_NA, HEAD_DIM))
        new_gqa_k.append(zc[:, C_CK:C_CV].reshape(BATCH, SEQ, KV_GQA, HEAD_DIM))
        new_gqa_v.append(zc[:, C_CV:D_IN].reshape(BATCH, SEQ, KV_GQA, HEAD_DIM))
        new_ret.append(st)
        oa_l = _lat_na(z, cache_na_k, cache_na_v, _na_bias_tables(na_rpb[l]), l)
        oc_l = _lat_gqa(z, cache_gqa_k, cache_gqa_v, l)
        oret_l, _ = _retention(z, state_ret[:, l], log_g, NCTX, DEC_BATCH, DEC_SEQ)
        x = _out_proj(x, oa_l, oret_l, z, oc_l, w_out_b, g1, NCTX, NLAT)
        h, top_idx, gates = _router(x, norm2_g[l].reshape(1, D_MODEL), sc2, sh2, router_w[l],
                                    router_b[l].reshape(1, N_EXPERTS))
        plan, slot_src, slot_dst = _dispatch(top_idx[:, :TOP_K])
        bgu = b_gate_up[l]
        y = _experts(h, plan, slot_src, slot_dst, w_gate_up,
                     bgu[:, 0::2].reshape(N_EXPERTS, 1, D_FF), bgu[:, 1::2].reshape(N_EXPERTS, 1, D_FF),
                     w_down, b_down[l].reshape(N_EXPERTS, 1, D_MODEL), l)
        x = _combine(x, y, gates, g2)
    y_prompt = _final_norm(x, final_g.reshape(1, D_MODEL), 0, NCTX).reshape(BATCH, SEQ, D_MODEL)
    y_sample = _final_norm(x, final_g.reshape(1, D_MODEL), NCTX, NLAT).reshape(DEC_BATCH, DEC_SEQ, D_MODEL)
    return (y_prompt, y_sample, jnp.stack(new_na_k, axis=1), jnp.stack(new_na_v, axis=1),
            jnp.stack(new_gqa_k, axis=1), jnp.stack(new_gqa_v, axis=1), jnp.stack(new_ret, axis=1))
```

```python
import functools

import jax
import jax.numpy as jnp
import numpy as np
from jax import lax
from jax.experimental import pallas as pl
from jax.experimental.pallas import tpu as pltpu

F32 = jnp.float32
BF16 = jnp.bfloat16

D_MODEL = 1024
BATCH = 32
SEQ = 256
DEPTH = 2
DEC_BATCH = 4
DEC_SEQ = 4096
PAST_LEN = 256
GRID_W = 64
HEAD_DIM = 64
H_NA = 4
H_RET = 4
H_GQA = 8
KV_GQA = 2
D_NA = H_NA * HEAD_DIM
D_RET = H_RET * HEAD_DIM
D_GQA = H_GQA * HEAD_DIM
D_KV = KV_GQA * HEAD_DIM
D_IN = 3 * D_NA + 4 * D_RET + D_GQA + 2 * D_KV
WIN_R = 8
WIN_C = 16
RET_CHUNK = 128
N_EXPERTS = 32
TOP_K = 4
D_FF = D_MODEL
SWIGLU_LIMIT = 7.0
SWIGLU_ALPHA = 1.702
MOE_BLOCK = 256
ROPE_THETA = 10000.0
EPS = 1e-6
ATTN_SCALE = HEAD_DIM ** -0.5

NCTX = BATCH * SEQ
NLAT = DEC_BATCH * DEC_SEQ
NTOK = NCTX + NLAT
N_COND = 8
NK = NTOK * TOP_K
N_MOE_BLOCKS = NK // MOE_BLOCK + N_EXPERTS
LANES = 128
NEG_BIG = -1e30

C_AQ, C_AK, C_AV = 0, D_NA, 2 * D_NA
C_BQ = 3 * D_NA
C_BK = C_BQ + D_RET
C_BV = C_BK + D_RET
C_BG = C_BV + D_RET
C_CQ = C_BG + D_RET
C_CK = C_CQ + D_GQA
C_CV = C_CK + D_KV

ROW_TILE = 512
NA_QROWS = 4
NA_BAND = 12
GQA_TQ = 256
VMEM_LIMIT = 56 * 1024 * 1024


def _cparams(n_axes):
    return pltpu.CompilerParams(dimension_semantics=("arbitrary",) * n_axes,
                                vmem_limit_bytes=VMEM_LIMIT)


def _mod_index(i, tile):
    nctx = NCTX // tile
    per_batch = DEC_SEQ // tile
    return jnp.where(i < nctx, 0, 1 + (i - nctx) // per_batch)


def _dot(a, b):
    return jnp.dot(a, b, preferred_element_type=F32)


def _dot_nt(a, b):
    return lax.dot_general(a, b, (((1,), (1,)), ((), ())), preferred_element_type=F32)


def _dot_tn(a, b):
    return lax.dot_general(a, b, (((0,), (0,)), ((), ())), preferred_element_type=F32)


def _mod_kernel(c_ref, w_ref, b_ref, o_ref):
    c = c_ref[...]
    s = c * jax.nn.sigmoid(c)
    o_ref[...] = jnp.dot(s, w_ref[...], preferred_element_type=F32,
                         precision=lax.Precision.HIGHEST) + b_ref[...]


def _modulation(cond, ada_w, ada_b):
    tn = 1536
    return pl.pallas_call(
        _mod_kernel, name="modulation",
        grid=(DEPTH, 6 * D_MODEL // tn),
        in_specs=[pl.BlockSpec((N_COND, D_MODEL), lambda l, j: (0, 0)),
                  pl.BlockSpec((None, D_MODEL, tn), lambda l, j: (l, 0, j)),
                  pl.BlockSpec((None, 1, tn), lambda l, j: (l, 0, j))],
        out_specs=pl.BlockSpec((None, N_COND, tn), lambda l, j: (l, 0, j)),
        out_shape=jax.ShapeDtypeStruct((DEPTH, N_COND, 6 * D_MODEL), F32),
        compiler_params=_cparams(2),
    )(cond, ada_w, ada_b.reshape(DEPTH, 1, 6 * D_MODEL))


def _rms_mod(x, g, sc, sh):
    ms = jnp.mean(x * x, axis=-1, keepdims=True)
    return (x * lax.rsqrt(ms + EPS) * g) * (1.0 + sc) + sh


def _two_head_rsqrt(x):
    sq = x * x
    left = lax.broadcasted_iota(jnp.int32, x.shape, 1) < HEAD_DIM
    s_left = jnp.sum(jnp.where(left, sq, 0.0), axis=-1, keepdims=True)
    s_right = jnp.sum(jnp.where(left, 0.0, sq), axis=-1, keepdims=True)
    ms = jnp.where(left, s_left, s_right) * (1.0 / HEAD_DIM)
    return lax.rsqrt(ms + EPS)


def _rope128(x, cos, sin_signed):
    first = (lax.broadcasted_iota(jnp.int32, x.shape, 1) & (HEAD_DIM // 2)) == 0
    rot = jnp.where(first, pltpu.roll(x, LANES - HEAD_DIM // 2, 1), pltpu.roll(x, HEAD_DIM // 2, 1))
    return x * cos + rot * sin_signed


def _softmax_pv(scores, values):
    m = functools.reduce(jnp.maximum, [jnp.max(s, axis=-1, keepdims=True) for s in scores])
    ps = [jnp.exp(s - m) for s in scores]
    denom = functools.reduce(lambda a, b: a + b, [jnp.sum(p, axis=-1, keepdims=True) for p in ps])
    o = functools.reduce(lambda a, b: a + b, [_dot(p.astype(BF16), v) for p, v in zip(ps, values)])
    return o / denom


def _head(x, h):
    return x[:, h * HEAD_DIM:(h + 1) * HEAD_DIM]


TOK_SUB = D_MODEL // LANES


def _store_token_tiles(ref, row0, x):
    n = x.shape[0]
    for s in range(TOK_SUB):
        ref[pl.ds(row0 + s, n, stride=TOK_SUB), :] = x[:, s * LANES:(s + 1) * LANES]


def _load_token_chunk(ref, row0, n, s):
    return ref[pl.ds(row0 + s, n, stride=TOK_SUB), :]


def _in_proj_kernel(x_ref, g_ref, sc_ref, sh_ref, w_ref, rc_ref, rs_ref, ac_ref, as_ref,
                    qg_ref, kg_ref, z_ref):
    hb = _rms_mod(x_ref[...], g_ref[...], sc_ref[...], sh_ref[...]).astype(BF16)

    def proj(c0, c1):
        return _dot(hb, w_ref[:, c0:c1])

    z_ref[:, C_AQ:C_AK] = proj(C_AQ, C_AK) * ATTN_SCALE
    z_ref[:, C_AK:C_BQ] = proj(C_AK, C_BQ)
    rc, rs = rc_ref[...], rs_ref[...]
    zq = proj(C_BQ, C_BK)
    zk = proj(C_BK, C_BV)
    for j in range(D_RET // LANES):
        sl = slice(j * LANES, (j + 1) * LANES)
        z_ref[:, C_BQ + j * LANES:C_BQ + (j + 1) * LANES] = _rope128(zq[:, sl], rc, rs)
        z_ref[:, C_BK + j * LANES:C_BK + (j + 1) * LANES] = _rope128(zk[:, sl], rc, rs) * ATTN_SCALE
    z_ref[:, C_BV:C_CQ] = proj(C_BV, C_CQ)
    ac, asn = ac_ref[...], as_ref[...]
    zc = proj(C_CQ, C_CK)
    for j in range(D_GQA // LANES):
        t = zc[:, j * LANES:(j + 1) * LANES]
        t = t * _two_head_rsqrt(t) * qg_ref[...]
        z_ref[:, C_CQ + j * LANES:C_CQ + (j + 1) * LANES] = _rope128(t, ac, asn) * ATTN_SCALE
    zkv = proj(C_CK, D_IN)
    t = zkv[:, :LANES]
    t = t * _two_head_rsqrt(t) * kg_ref[...]
    z_ref[:, C_CK:C_CV] = _rope128(t, ac, asn)
    z_ref[:, C_CV:D_IN] = zkv[:, LANES:]


def _in_proj(x, g, sc, sh, w_b, tabs, qg, kg):
    tm = ROW_TILE
    nctx = NCTX // tm
    per_seq = DEC_SEQ // tm

    def tab_idx(i):
        return (jnp.where(i < nctx, per_seq, (i - nctx) % per_seq), 0)

    row = lambda i: (i, 0)
    fixed = lambda i: (0, 0)
    mod = lambda i: (_mod_index(i, tm), 0, 0)
    tab_spec = pl.BlockSpec((tm, LANES), tab_idx)
    return pl.pallas_call(
        _in_proj_kernel, name="in_proj",
        grid=(NTOK // tm,),
        in_specs=[pl.BlockSpec((tm, D_MODEL), row),
                  pl.BlockSpec((1, D_MODEL), fixed),
                  pl.BlockSpec((None, 1, D_MODEL), mod),
                  pl.BlockSpec((None, 1, D_MODEL), mod),
                  pl.BlockSpec((D_MODEL, D_IN), fixed),
                  tab_spec, tab_spec, tab_spec, tab_spec,
                  pl.BlockSpec((1, LANES), fixed),
                  pl.BlockSpec((1, LANES), fixed)],
        out_specs=pl.BlockSpec((tm, D_IN), row),
        out_shape=jax.ShapeDtypeStruct((NTOK, D_IN), F32),
        compiler_params=_cparams(1),
    )(x, g, sc, sh, w_b, *tabs, qg, kg)


def _rope_tables():
    t = np.arange(DEC_SEQ)
    inv_ret = 1.0 / (ROPE_THETA ** np.linspace(0.0, 1.0, HEAD_DIM // 2, dtype=np.float32))
    ang_ret = t.astype(np.float32)[:, None] * inv_ret.astype(np.float32)
    n = HEAD_DIM // 4
    inv_ax = (ROPE_THETA ** (-np.arange(n, dtype=np.float32) / n)).astype(np.float32)
    row = (t // GRID_W).astype(np.float32)
    col = (t % GRID_W).astype(np.float32)
    ang_ax = np.concatenate([row[:, None] * inv_ax, col[:, None] * inv_ax], axis=-1)

    def tables(ang):
        ang = jnp.asarray(ang, F32)
        cos, sin = jnp.cos(ang), jnp.sin(ang)
        cos2 = jnp.tile(jnp.concatenate([cos, cos], axis=-1), (1, LANES // HEAD_DIM))
        sin2 = jnp.tile(jnp.concatenate([-sin, sin], axis=-1), (1, LANES // HEAD_DIM))
        cos2 = jnp.concatenate([cos2, jnp.ones((ROW_TILE, LANES), F32)], axis=0)
        sin2 = jnp.concatenate([sin2, jnp.zeros((ROW_TILE, LANES), F32)], axis=0)
        return cos2, sin2

    return (*tables(ang_ret), *tables(ang_ax))


def _ctx_attn_kernel(aq_ref, ak_ref, av_ref, cq0_ref, cq1_ref, ck_ref, cv_ref, oa_ref, oc_ref):
    aq = aq_ref[...].astype(BF16)
    ak = ak_ref[...].astype(BF16)
    av = av_ref[...].astype(BF16)
    outs = []
    for h in range(H_NA):
        s = _dot_nt(_head(aq, h), _head(ak, h))
        outs.append(_softmax_pv([s], [_head(av, h)]))
    oa_ref[...] = jnp.concatenate(outs, axis=-1)
    ck = ck_ref[...].astype(BF16)
    cv = cv_ref[...].astype(BF16)
    group = H_GQA // KV_GQA
    outs = []
    for g, cq_ref in enumerate((cq0_ref, cq1_ref)):
        cq = cq_ref[...].astype(BF16)
        qs = jnp.concatenate([_head(cq, j) for j in range(group)], axis=0)
        o = _softmax_pv([_dot_nt(qs, _head(ck, g))], [_head(cv, g)])
        outs += [o[j * SEQ:(j + 1) * SEQ] for j in range(group)]
    oc_ref[...] = jnp.concatenate(outs, axis=-1)


def _ctx_attention(z):
    col = lambda c, w: pl.BlockSpec((SEQ, w), lambda b: (b, c // w))
    return pl.pallas_call(
        _ctx_attn_kernel, name="ctx_attn",
        grid=(BATCH,),
        in_specs=[col(C_AQ, D_NA), col(C_AK, D_NA), col(C_AV, D_NA),
                  col(C_CQ, D_NA), col(C_CQ + D_NA, D_NA), col(C_CK, D_KV), col(C_CV, D_KV)],
        out_specs=[pl.BlockSpec((SEQ, D_NA), lambda b: (b, 0)),
                   pl.BlockSpec((SEQ, D_GQA), lambda b: (b, 0))],
        out_shape=[jax.ShapeDtypeStruct((NCTX, D_NA), F32),
                   jax.ShapeDtypeStruct((NCTX, D_GQA), F32)],
        compiler_params=_cparams(1),
    )(z, z, z, z, z, z, z)


def _lat_gqa_kernel(cq0_ref, cq1_ref, ck_ref, cv_ref, kc_ref, vc_ref, oc_ref):
    ck = ck_ref[...].astype(BF16)
    cv = cv_ref[...].astype(BF16)
    kc = kc_ref[...].astype(BF16)
    vc = vc_ref[...].astype(BF16)
    group = H_GQA // KV_GQA
    outs = []
    for g, cq_ref in enumerate((cq0_ref, cq1_ref)):
        cq = cq_ref[...].astype(BF16)
        qs = jnp.concatenate([_head(cq, j) for j in range(group)], axis=0)
        o = _softmax_pv([_dot_nt(qs, _head(kc, g)), _dot_nt(qs, _head(ck, g))],
                        [_head(vc, g), _head(cv, g)])
        outs += [o[j * GQA_TQ:(j + 1) * GQA_TQ] for j in range(group)]
    oc_ref[...] = jnp.concatenate(outs, axis=-1)


def _lat_gqa(z, cache_k, cache_v, layer):
    nq = DEC_SEQ // GQA_TQ
    q0 = NCTX // GQA_TQ
    s0 = NCTX // DEC_SEQ
    qspec = lambda c: pl.BlockSpec((GQA_TQ, D_NA), lambda b, i: (q0 + b * nq + i, c // D_NA))
    kvspec = lambda c: pl.BlockSpec((DEC_SEQ, D_KV), lambda b, i: (s0 + b, c // D_KV))
    cspec = pl.BlockSpec((None, None, PAST_LEN, D_KV), lambda b, i: (b, layer, 0, 0))
    return pl.pallas_call(
        _lat_gqa_kernel, name="lat_gqa",
        grid=(DEC_BATCH, nq),
        in_specs=[qspec(C_CQ), qspec(C_CQ + D_NA), kvspec(C_CK), kvspec(C_CV), cspec, cspec],
        out_specs=pl.BlockSpec((GQA_TQ, D_GQA), lambda b, i: (b * nq + i, 0)),
        out_shape=jax.ShapeDtypeStruct((NLAT, D_GQA), F32),
        compiler_params=_cparams(2),
    )(z, z, z, z, cache_k, cache_v)


def _na_band_start(i):
    return jnp.clip(NA_QROWS * i - WIN_R // 2, 0, GRID_W - NA_BAND)


def _lat_na_kernel(q_ref, k_ref, v_ref, kc_ref, vc_ref, bias_ref, o_ref):
    i = pl.program_id(1)
    start = pl.multiple_of(_na_band_start(i) * GRID_W, GRID_W)
    q = q_ref[...].astype(BF16)
    kb = k_ref[pl.ds(start, NA_BAND * GRID_W), :].astype(BF16)
    vb = v_ref[pl.ds(start, NA_BAND * GRID_W), :].astype(BF16)
    kc = kc_ref[...].astype(BF16)
    vc = vc_ref[...].astype(BF16)
    outs = []
    for h in range(H_NA):
        qh = _head(q, h)
        s_loc = _dot_nt(qh, _head(kb, h)) + bias_ref[h]
        s_ctx = _dot_nt(qh, _head(kc, h))
        outs.append(_softmax_pv([s_loc, s_ctx], [_head(vb, h), _head(vc, h)]))
    o_ref[...] = jnp.concatenate(outs, axis=-1)


def _na_bias_tables(rpb):
    nq, nk = NA_QROWS * GRID_W, NA_BAND * GRID_W
    rows = DEC_SEQ // GRID_W
    nblk = rows // NA_QROWS
    n_rb, n_cb = 2 * WIN_R - 1, 2 * WIN_C - 1
    row_sel = np.zeros((3, NA_QROWS, NA_BAND, n_rb), np.float32)
    row_ok = np.zeros((3, NA_QROWS, NA_BAND), bool)
    for kind, blk in enumerate((0, 1, nblk - 1)):
        bs = int(np.clip(NA_QROWS * blk - WIN_R // 2, 0, rows - NA_BAND))
        for jr in range(NA_QROWS):
            r = NA_QROWS * blk + jr
            rs = int(np.clip(r - WIN_R // 2, 0, rows - WIN_R))
            for bi in range(NA_BAND):
                ri = bs + bi
                if rs <= ri < rs + WIN_R:
                    row_ok[kind, jr, bi] = True
                    row_sel[kind, jr, bi, ri - r + WIN_R - 1] = 1.0
    c = np.arange(GRID_W)
    cs = np.clip(c - WIN_C // 2, 0, GRID_W - WIN_C)
    col_ok = (c[None, :] >= cs[:, None]) & (c[None, :] < cs[:, None] + WIN_C)
    col_sel = np.zeros((n_cb, GRID_W, GRID_W), np.float32)
    qc, kc = np.nonzero(col_ok)
    col_sel[kc - qc + WIN_C - 1, qc, kc] = 1.0
    hi = lax.Precision.HIGHEST
    by_row = jnp.einsum('kjba,hax->hkjbx', row_sel, rpb, precision=hi)
    bias = jnp.einsum('hkjbx,xcz->khjcbz', by_row, col_sel, precision=hi)
    ok = row_ok[:, None, :, None, :, None] & col_ok[None, None, None, :, None, :]
    return jnp.where(ok, bias, NEG_BIG).reshape(3, H_NA, nq, nk).astype(F32)


def _lat_na(z, cache_k, cache_v, bias, layer):
    nq_rows = NA_QROWS * GRID_W
    nblk = DEC_SEQ // nq_rows
    q0 = NCTX // nq_rows
    s0 = NCTX // DEC_SEQ
    kvspec = lambda c: pl.BlockSpec((DEC_SEQ, D_NA), lambda b, i: (s0 + b, c // D_NA))
    cspec = pl.BlockSpec((None, None, PAST_LEN, D_NA), lambda b, i: (b, layer, 0, 0))
    kind = lambda b, i: (jnp.where(i == 0, 0, jnp.where(i == nblk - 1, 2, 1)), 0, 0, 0)
    return pl.pallas_call(
        _lat_na_kernel, name="lat_na",
        grid=(DEC_BATCH, nblk),
        in_specs=[pl.BlockSpec((nq_rows, D_NA), lambda b, i: (q0 + b * nblk + i, 0)),
                  kvspec(C_AK), kvspec(C_AV), cspec, cspec,
                  pl.BlockSpec((None, H_NA, nq_rows, NA_BAND * GRID_W), kind)],
        out_specs=pl.BlockSpec((nq_rows, D_NA), lambda b, i: (b * nblk + i, 0)),
        out_shape=jax.ShapeDtypeStruct((NLAT, D_NA), F32),
        compiler_params=_cparams(2),
    )(z, z, z, cache_k, cache_v, bias)


def _ret_kernel(lg_ref, q_ref, k_ref, v_ref, s0_ref, lgl_ref, o_ref, sfin_ref, s_scr, *, n_chunks):
    d = pl.program_id(1)
    C = RET_CHUNK
    s_scr[...] = s0_ref[...].astype(F32)
    fwd = d == 0
    row = lax.broadcasted_iota(jnp.int32, (C, 1), 0).astype(F32)
    pos_q = jnp.where(fwd, row + 1.0, C - row)
    pos_k = jnp.where(fwd, C - 1.0 - row, row)
    lgl = lgl_ref[...]
    q_decay = jnp.exp(lgl * pos_q)
    k_decay = jnp.exp(lgl * pos_k)
    chunk_decay = jnp.exp(lgl * float(C))
    ii = lax.broadcasted_iota(jnp.int32, (C, C), 0)
    jj = lax.broadcasted_iota(jnp.int32, (C, C), 1)
    delta = jnp.where(fwd, ii - jj, jj - ii).astype(F32)
    masks = [jnp.where(delta >= 0.0, jnp.exp(lg_ref[d, h] * jnp.maximum(delta, 0.0)), 0.0)
             for h in range(H_RET)]

    def step(c, carry):
        r0 = pl.multiple_of(jnp.where(fwd, c, n_chunks - 1 - c) * C, C)
        q = q_ref[pl.ds(r0, C), :]
        k = k_ref[pl.ds(r0, C), :]
        vb = v_ref[pl.ds(r0, C), :].astype(BF16)
        kd = (k * k_decay).astype(BF16)
        qb = q.astype(BF16)
        kb = k.astype(BF16)
        outs = []
        for h in range(H_RET):
            qh, kh, vh = _head(qb, h), _head(kb, h), _head(vb, h)
            inner = _dot((_dot_nt(qh, kh) * masks[h]).astype(BF16), vh)
            s = s_scr[h]
            cross = _dot(qh, s.astype(BF16)) * _head(q_decay, h)
            s_scr[h] = s * _head(chunk_decay, h) + _dot_tn(_head(kd, h), vh)
            outs.append(inner + cross)
        o_ref[pl.ds(r0, C), :] = jnp.concatenate(outs, axis=-1)
        return carry

    lax.fori_loop(0, n_chunks, step, 0)
    sfin_ref[...] = s_scr[...]


def _retention(z, s0, log_g, row0, n_batch, seq):
    n = seq // RET_CHUNK
    b0 = row0 // seq
    lgl = jnp.repeat(log_g, HEAD_DIM, axis=-1).reshape(2, 1, D_RET)
    zspec = lambda col: pl.BlockSpec((seq, D_RET), lambda b, d, lg: (b0 + b, col // D_RET))
    sspec = pl.BlockSpec((None, None, H_RET, HEAD_DIM, HEAD_DIM), lambda b, d, lg: (b, d, 0, 0, 0))
    return pl.pallas_call(
        functools.partial(_ret_kernel, n_chunks=n), name="retention",
        grid_spec=pltpu.PrefetchScalarGridSpec(
            num_scalar_prefetch=1,
            grid=(n_batch, 2),
            in_specs=[zspec(C_BQ), zspec(C_BK), zspec(C_BV), sspec,
                      pl.BlockSpec((None, 1, D_RET), lambda b, d, lg: (d, 0, 0))],
            out_specs=[pl.BlockSpec((None, seq, D_RET), lambda b, d, lg: (d, b, 0)),
                       sspec],
            scratch_shapes=[pltpu.VMEM((H_RET, HEAD_DIM, HEAD_DIM), F32)]),
        out_shape=[jax.ShapeDtypeStruct((2, n_batch * seq, D_RET), F32),
                   jax.ShapeDtypeStruct((n_batch, 2, H_RET, HEAD_DIM, HEAD_DIM), F32)],
        compiler_params=_cparams(2),
    )(log_g, z, z, z, s0, lgl)


def _out_proj_kernel(x_ref, oa_ref, of_ref, ob_ref, bg_ref, oc_ref, w_ref, gate_ref, y_ref):
    ob = of_ref[...] + ob_ref[...]
    bg = bg_ref[...]
    parts = []
    for j in range(D_RET // LANES):
        t = ob[:, j * LANES:(j + 1) * LANES]
        g = bg[:, j * LANES:(j + 1) * LANES]
        parts.append(t * _two_head_rsqrt(t) * (g * jax.nn.sigmoid(g)))
    obn = jnp.concatenate(parts, axis=-1).astype(BF16)
    o = (_dot(oa_ref[...].astype(BF16), w_ref[0:D_NA, :])
         + _dot(obn, w_ref[D_NA:D_NA + D_RET, :])
         + _dot(oc_ref[...].astype(BF16), w_ref[D_NA + D_RET:, :]))
    y_ref[...] = x_ref[...] + gate_ref[...] * o


def _out_proj(x, oa, o_ret, z, oc, w_b, gate, row0, rows):
    tm = ROW_TILE
    r0 = row0 // tm
    loc = lambda i: (i, 0)
    return pl.pallas_call(
        _out_proj_kernel, name="out_proj",
        grid=(rows // tm,),
        in_specs=[pl.BlockSpec((tm, D_MODEL), lambda i: (r0 + i, 0)),
                  pl.BlockSpec((tm, D_NA), loc),
                  pl.BlockSpec((None, tm, D_RET), lambda i: (0, i, 0)),
                  pl.BlockSpec((None, tm, D_RET), lambda i: (1, i, 0)),
                  pl.BlockSpec((tm, D_RET), lambda i: (r0 + i, C_BG // D_RET)),
                  pl.BlockSpec((tm, D_GQA), loc),
                  pl.BlockSpec((D_MODEL, D_MODEL), lambda i: (0, 0)),
                  pl.BlockSpec((None, 1, D_MODEL), lambda i: (_mod_index(r0 + i, tm), 0, 0))],
        out_specs=pl.BlockSpec((tm, D_MODEL), lambda i: (r0 + i, 0)),
        out_shape=jax.ShapeDtypeStruct((NTOK, D_MODEL), F32),
        input_output_aliases={0: 0},
        compiler_params=_cparams(1),
    )(x, oa, o_ret, o_ret, z, oc, w_b, gate)


def _router_kernel(x_ref, g_ref, sc_ref, sh_ref, w_ref, b_ref, h_ref, idx_ref, gate_ref):
    h = _rms_mod(x_ref[...], g_ref[...], sc_ref[...], sh_ref[...])
    _store_token_tiles(h_ref, 0, h)
    w = w_ref[...]
    w_hi = w.astype(BF16)
    w_lo = (w - w_hi.astype(F32)).astype(BF16)
    h_hi = h.astype(BF16)
    h_lo = (h - h_hi.astype(F32)).astype(BF16)
    logits = _dot(h_hi, w_hi) + (_dot(h_hi, w_lo) + _dot(h_lo, w_hi)) + b_ref[...]
    lane = lax.broadcasted_iota(jnp.int32, logits.shape, 1)
    out_lane = lax.broadcasted_iota(jnp.int32, idx_ref.shape, 1)
    idx_out = jnp.zeros(idx_ref.shape, jnp.int32)
    val_out = jnp.zeros(gate_ref.shape, F32)
    top = None
    denom = None
    for k in range(TOP_K):
        m = jnp.max(logits, axis=-1, keepdims=True)
        i = jnp.min(jnp.where(logits == m, lane, N_EXPERTS), axis=-1, keepdims=True)
        logits = jnp.where(lane == i, -jnp.inf, logits)
        if k == 0:
            top = m
        e = jnp.exp(m - top)
        denom = e if k == 0 else denom + e
        idx_out = jnp.where(out_lane == k, i, idx_out)
        val_out = jnp.where(out_lane == k, e, val_out)
    idx_ref[...] = idx_out
    gate_ref[...] = val_out / denom


def _router(x, g, sc, sh, rw, rb):
    tm = ROW_TILE
    row = lambda i: (i, 0)
    fixed = lambda i: (0, 0)
    mod = lambda i: (_mod_index(i, tm), 0, 0)
    return pl.pallas_call(
        _router_kernel, name="router",
        grid=(NTOK // tm,),
        in_specs=[pl.BlockSpec((tm, D_MODEL), row),
                  pl.BlockSpec((1, D_MODEL), fixed),
                  pl.BlockSpec((None, 1, D_MODEL), mod),
                  pl.BlockSpec((None, 1, D_MODEL), mod),
                  pl.BlockSpec((D_MODEL, N_EXPERTS), fixed),
                  pl.BlockSpec((1, N_EXPERTS), fixed)],
        out_specs=[pl.BlockSpec((tm * TOK_SUB, LANES), row),
                   pl.BlockSpec((tm, LANES), row),
                   pl.BlockSpec((tm, LANES), row)],
        out_shape=[jax.ShapeDtypeStruct((NTOK * TOK_SUB, LANES), F32),
                   jax.ShapeDtypeStruct((NTOK, LANES), jnp.int32),
                   jax.ShapeDtypeStruct((NTOK, LANES), F32)],
        compiler_params=_cparams(1),
    )(x, g, sc, sh, rw, rb)


PAD_ROW0 = TOP_K * NTOK
Y_ROWS = PAD_ROW0 + 2 * MOE_BLOCK
GU_CHUNK = 2 * LANES


def _expert_kernel(be_ref, nv_ref, ne_ref, ws_ref, src0_ref, srcn_ref, dst_ref, h_hbm, wgu_hbm, bg_ref,
                   bl_ref, wd_hbm, bd_ref, y_hbm, xbuf, ybuf, wg_s, wl_s, wd_s, t_scr, wgu_buf, wd_buf,
                   gsem, ssem, wsem, *, layer):
    j = pl.program_id(0)
    nv = nv_ref[0]
    cur = j % 2
    nxt = 1 - cur
    half = MOE_BLOCK * TOK_SUB

    def tile(ref, t):
        start = t * TOK_SUB
        if not isinstance(start, int):
            start = pl.multiple_of(start, TOK_SUB)
        return ref.at[pl.ds(start, TOK_SUB), :]

    def issue_gather(src_ref, slot):
        for r in range(MOE_BLOCK):
            pltpu.make_async_copy(tile(h_hbm, src_ref[0, r]), tile(xbuf, slot * MOE_BLOCK + r),
                                  gsem.at[slot]).start()

    def issue_scatter(slot, dst_of):
        for r in range(MOE_BLOCK):
            pltpu.make_async_copy(tile(ybuf, slot * MOE_BLOCK + r), tile(y_hbm, dst_of(r)),
                                  ssem.at[slot]).start()

    def wait_half(buf, sem, slot):
        whole = buf.at[pl.ds(pl.multiple_of(slot * half, half), half), :]
        pltpu.make_async_copy(whole, whole, sem.at[slot]).wait()

    wait_gather = functools.partial(wait_half, xbuf, gsem)
    wait_scatter = functools.partial(wait_half, ybuf, ssem)

    def weight_copies(e, p):
        return (pltpu.make_async_copy(wgu_hbm.at[layer, e], wgu_buf.at[p], wsem.at[p, 0]),
                pltpu.make_async_copy(wd_hbm.at[layer, e], wd_buf.at[p], wsem.at[p, 1]))

    @pl.when(j == 0)
    def _():
        issue_gather(src0_ref, 0)
        ybuf[...] = jnp.zeros(ybuf.shape, F32)
        for slot in range(2):
            issue_scatter(slot, lambda r, slot=slot: PAD_ROW0 + slot * MOE_BLOCK + r)
        for cp in weight_copies(be_ref[0], 0):
            cp.start()

    changed =jnp.logical_or(j == 0, be_ref[j] != be_ref[jnp.maximum(j - 1, 0)])

    @pl.when(jnp.logical_and(j < nv, changed))
    def _():
        p = ws_ref[j]
        for cp in weight_copies(be_ref[j], p):
            cp.wait()

        @pl.when(ne_ref[j] >= 0)
        def _():
            for cp in weight_copies(ne_ref[j], 1 - p):
                cp.start()

        wgu_ref = wgu_buf.at[p]
        wd_ref = wd_buf.at[p]
        for c in range(2 * D_FF // GU_CHUNK):
            t = wgu_ref[:, c * GU_CHUNK:(c + 1) * GU_CHUNK].T
            for kk in range(D_MODEL // LANES):
                t_scr[kk] = t[:, kk * LANES:(kk + 1) * LANES]
            rows = slice(c * GU_CHUNK // 2, (c + 1) * GU_CHUNK // 2)
            for kk in range(D_MODEL // LANES):
                cols = slice(kk * LANES, (kk + 1) * LANES)
                wg_s[rows, cols] = t_scr[kk, pl.ds(0, GU_CHUNK // 2, stride=2), :].astype(BF16)
                wl_s[rows, cols] = t_scr[kk, pl.ds(1, GU_CHUNK // 2, stride=2), :].astype(BF16)
        wd_s[...] = wd_ref[...].astype(BF16)

    @pl.when(j < nv)
    def _():
        wait_scatter(cur)
        wait_gather(cur)
        issue_gather(srcn_ref, nxt)
        base = pl.multiple_of(cur * half, half)
        xb = jnp.concatenate([_load_token_chunk(xbuf, base, MOE_BLOCK, s).astype(BF16)
                              for s in range(TOK_SUB)], axis=-1)
        glu = jnp.minimum(_dot_nt(xb, wg_s[...]) + bg_ref[...], SWIGLU_LIMIT)
        lin = jnp.clip(_dot_nt(xb, wl_s[...]) + bl_ref[...], -SWIGLU_LIMIT, SWIGLU_LIMIT)
        act = glu * jax.nn.sigmoid(SWIGLU_ALPHA * glu) * (lin + 1.0)
        _store_token_tiles(ybuf, base, _dot(act.astype(BF16), wd_s[...]) + bd_ref[...])
        issue_scatter(cur, lambda r: dst_ref[0, r])

    @pl.when(j == nv - 1)
    def _():
        wait_gather(nxt)
        wait_scatter(nxt)
        wait_scatter(cur)


def _experts(h, plan, slot_src, slot_dst, wgu, bg, bl, wd, bd, layer):
    wmap = lambda j, be, *_: (be[j], 0, 0)
    idx_spec = lambda fn: pl.BlockSpec((None, 1, MOE_BLOCK), fn, memory_space=pltpu.SMEM)
    return pl.pallas_call(
        functools.partial(_expert_kernel, layer=layer),
        grid_spec=pltpu.PrefetchScalarGridSpec(
            num_scalar_prefetch=len(plan),
            grid=(N_MOE_BLOCKS,),
            in_specs=[idx_spec(lambda j, *_: (0, 0, 0)),
                      idx_spec(lambda j, *_: (jnp.minimum(j + 1, N_MOE_BLOCKS - 1), 0, 0)),
                      idx_spec(lambda j, *_: (j, 0, 0)),
                      pl.BlockSpec(memory_space=pl.ANY),
                      pl.BlockSpec(memory_space=pl.ANY),
                      pl.BlockSpec((None, 1, D_FF), wmap),
                      pl.BlockSpec((None, 1, D_FF), wmap),
                      pl.BlockSpec(memory_space=pl.ANY),
                      pl.BlockSpec((None, 1, D_MODEL), wmap)],
            out_specs=pl.BlockSpec(memory_space=pl.ANY),
            scratch_shapes=[pltpu.VMEM((2 * MOE_BLOCK * TOK_SUB, LANES), F32),
                            pltpu.VMEM((2 * MOE_BLOCK * TOK_SUB, LANES), F32),
                            pltpu.VMEM((D_FF, D_MODEL), BF16),
                            pltpu.VMEM((D_FF, D_MODEL), BF16),
                            pltpu.VMEM((D_FF, D_MODEL), BF16),
                            pltpu.VMEM((D_MODEL // LANES, GU_CHUNK, LANES), F32),
                            pltpu.VMEM((2, D_MODEL, 2 * D_FF), F32),
                            pltpu.VMEM((2, D_FF, D_MODEL), F32),
                            pltpu.SemaphoreType.DMA((2,)),
                            pltpu.SemaphoreType.DMA((2,)),
                            pltpu.SemaphoreType.DMA((2, 2))]),
        out_shape=jax.ShapeDtypeStruct((Y_ROWS * TOK_SUB, LANES), F32),
        compiler_params=_cparams(1),
        name="experts",
    )(*plan, slot_src, slot_src, slot_dst, h, wgu, bg, bl, wd, bd)


def _dispatch(top_idx):
    flat_e = top_idx.reshape(-1)
    order = jnp.argsort(flat_e, stable=True).astype(jnp.int32)
    experts = jnp.arange(N_EXPERTS, dtype=jnp.int32)
    counts = jnp.sum((flat_e[:, None] == experts[None, :]).astype(jnp.int32), axis=0)
    padded = ((counts + MOE_BLOCK - 1) // MOE_BLOCK) * MOE_BLOCK
    pad_end = jnp.cumsum(padded)
    pad_start = pad_end - padded
    start = jnp.cumsum(counts) - counts
    block0 = jnp.arange(N_MOE_BLOCKS, dtype=jnp.int32) * MOE_BLOCK
    n_valid = (pad_end[-1] // MOE_BLOCK).astype(jnp.int32).reshape(1)
    used = block0 < pad_end[-1]
    block_e = jnp.sum((pad_end[None, :] <= block0[:, None]).astype(jnp.int32), axis=1)
    block_e = jnp.minimum(block_e, N_EXPERTS - 1)
    onehot = block_e[:, None] == experts[None, :]
    pick = lambda v: jnp.sum(jnp.where(onehot, v[None, :], 0), axis=1)
    offset = block0 - pick(pad_start)
    first = pick(start) + offset
    left = jnp.where(used, pick(counts) - offset, 0)
    r = jnp.arange(MOE_BLOCK, dtype=jnp.int32)[None, :]
    real = r < left[:, None]
    pos = jnp.clip(first[:, None] + r, 0, NK - 1)
    aid = jnp.take(order, pos, axis=0)
    pad_dst = PAD_ROW0 + ((block0 // MOE_BLOCK) % 2)[:, None] * MOE_BLOCK + r
    slot_src = jnp.where(real, aid // TOP_K, 0)
    slot_dst = jnp.where(real, (aid % TOP_K) * NTOK + aid // TOP_K, pad_dst)
    run_first = jnp.logical_and(used, offset == 0)
    w_half = ((jnp.cumsum(run_first.astype(jnp.int32)) - 1) % 2).astype(jnp.int32)
    next_block = pick(pad_end) // MOE_BLOCK
    next_e = jnp.take(block_e, jnp.minimum(next_block, N_MOE_BLOCKS - 1), axis=0)
    next_e = jnp.where(jnp.logical_and(used, next_block < n_valid[0]), next_e, -1).astype(jnp.int32)
    last_e = jnp.sum(jnp.where(block0 == (n_valid[0] - 1) * MOE_BLOCK, block_e, 0))
    block_e = jnp.where(used, block_e, last_e).astype(jnp.int32)
    shape = (N_MOE_BLOCKS, 1, MOE_BLOCK)
    return (block_e, n_valid, next_e, w_half), slot_src.reshape(shape), slot_dst.reshape(shape)


def _combine_kernel(x_ref, *refs):
    y_refs, (gate_ref, mod_ref, o_ref) = refs[:TOP_K], refs[TOP_K:]
    gates = gate_ref[...]
    n = x_ref.shape[0]
    for s in range(TOK_SUB):
        cols = slice(s * LANES, (s + 1) * LANES)
        acc = gates[:, 0:1] * _load_token_chunk(y_refs[0], 0, n, s)
        for k in range(1, TOP_K):
            acc = acc + gates[:, k:k + 1] * _load_token_chunk(y_refs[k], 0, n, s)
        o_ref[:, cols] = x_ref[:, cols] + mod_ref[:, cols] * acc


def _combine(x, y, gates, gate2):
    tm = ROW_TILE
    row = lambda i: (i, 0)
    plane = lambda k: pl.BlockSpec((tm * TOK_SUB, LANES), lambda i: (k * (NTOK // tm) + i, 0))
    return pl.pallas_call(
        _combine_kernel,
        grid=(NTOK // tm,),
        in_specs=[pl.BlockSpec((tm, D_MODEL), row)] + [plane(k) for k in range(TOP_K)]
        + [pl.BlockSpec((tm, LANES), row),
           pl.BlockSpec((None, 1, D_MODEL), lambda i: (_mod_index(i, tm), 0, 0))],
        out_specs=pl.BlockSpec((tm, D_MODEL), row),
        out_shape=jax.ShapeDtypeStruct((NTOK, D_MODEL), F32),
        input_output_aliases={0: 0},
        compiler_params=_cparams(1),
        name="combine",
    )(x, *([y] * TOP_K), gates, gate2)


def _final_norm_kernel(x_ref, g_ref, o_ref):
    x = x_ref[...]
    ms = jnp.mean(x * x, axis=-1, keepdims=True)
    o_ref[...] = x * lax.rsqrt(ms + EPS) * g_ref[...]


def _final_norm(x, g, row0, rows):
    tm = ROW_TILE
    r0 = row0 // tm
    return pl.pallas_call(
        _final_norm_kernel, name="final_norm",
        grid=(rows // tm,),
        in_specs=[pl.BlockSpec((tm, D_MODEL), lambda i: (r0 + i, 0)),
                  pl.BlockSpec((1, D_MODEL), lambda i: (0, 0))],
        out_specs=pl.BlockSpec((tm, D_MODEL), lambda i: (i, 0)),
        out_shape=jax.ShapeDtypeStruct((rows, D_MODEL), F32),
        compiler_params=_cparams(1),
    )(x, g)


def kernel(x_prompt, x_sample, cache_na_k, cache_na_v, cache_gqa_k, cache_gqa_v, state_ret, c, c_ctx,
           ada_w, ada_b, norm1_g, norm2_g, w_in, w_out, na_rpb, ret_decay_logit, q_norm_g, k_norm_g,
           router_w, router_b, w_gate_up, b_gate_up, w_down, b_down, final_g):
    x = jnp.concatenate([x_prompt.reshape(NCTX, D_MODEL), x_sample.reshape(NLAT, D_MODEL)], axis=0)
    cond = jnp.zeros((N_COND, D_MODEL), F32).at[0].set(c_ctx).at[1:1 + DEC_BATCH].set(c)
    mod = _modulation(cond, ada_w, ada_b)
    tabs = _rope_tables()
    cache_na_k = cache_na_k.reshape(DEC_BATCH, DEPTH, PAST_LEN, D_NA)
    cache_na_v = cache_na_v.reshape(DEC_BATCH, DEPTH, PAST_LEN, D_NA)
    cache_gqa_k = cache_gqa_k.reshape(DEC_BATCH, DEPTH, PAST_LEN, D_KV)
    cache_gqa_v = cache_gqa_v.reshape(DEC_BATCH, DEPTH, PAST_LEN, D_KV)
    zero_state = jnp.zeros((BATCH, 2, H_RET, HEAD_DIM, HEAD_DIM), F32)
    two_heads = lambda g: jnp.tile(g, LANES // HEAD_DIM).reshape(1, LANES)
    new_na_k, new_na_v, new_gqa_k, new_gqa_v, new_ret = [], [], [], [], []
    for l in range(DEPTH):
        sh1, sc1, g1, sh2, sc2, g2 = [m.reshape(N_COND, 1, D_MODEL) for m in jnp.split(mod[l], 6, axis=-1)]
        z = _in_proj(x, norm1_g[l].reshape(1, D_MODEL), sc1, sh1, w_in[l].astype(BF16), tabs,
                     two_heads(q_norm_g[l]), two_heads(k_norm_g[l]))
        log_g = jax.nn.log_sigmoid(ret_decay_logit[l].astype(F32))
        w_out_b = w_out[l].astype(BF16)
        oa_c, oc_c = _ctx_attention(z)
        oret_c, st = _retention(z, zero_state, log_g, 0, BATCH, SEQ)
        x = _out_proj(x, oa_c, oret_c, z, oc_c, w_out_b, g1, 0, NCTX)
        zc = z[:NCTX]
        new_na_k.append(zc[:, C_AK:C_AV].reshape(BATCH, SEQ, H_NA, HEAD_DIM))
        new_na_v.append(zc[:, C_AV:C_BQ].reshape(BATCH, SEQ, H_NA, HEAD_DIM))
        new_gqa_k.append(zc[:, C_CK:C_CV].reshape(BATCH, SEQ, KV_GQA, HEAD_DIM))
        new_gqa_v.append(zc[:, C_CV:D_IN].reshape(BATCH, SEQ, KV_GQA, HEAD_DIM))
        new_ret.append(st)
        oa_l = _lat_na(z, cache_na_k, cache_na_v, _na_bias_tables(na_rpb[l]), l)
        oc_l = _lat_gqa(z, cache_gqa_k, cache_gqa_v, l)
        oret_l, _ = _retention(z, state_ret[:, l], log_g, NCTX, DEC_BATCH, DEC_SEQ)
        x = _out_proj(x, oa_l, oret_l, z, oc_l, w_out_b, g1, NCTX, NLAT)
        h, top_idx, gates = _router(x, norm2_g[l].reshape(1, D_MODEL), sc2, sh2, router_w[l],
                                    router_b[l].reshape(1, N_EXPERTS))
        plan, slot_src, slot_dst = _dispatch(top_idx[:, :TOP_K])
        bgu = b_gate_up[l]
        y = _experts(h, plan, slot_src, slot_dst, w_gate_up,
                     bgu[:, 0::2].reshape(N_EXPERTS, 1, D_FF), bgu[:, 1::2].reshape(N_EXPERTS, 1, D_FF),
                     w_down, b_down[l].reshape(N_EXPERTS, 1, D_MODEL), l)
        x = _combine(x, y, gates, g2)
    y_prompt = _final_norm(x, final_g.reshape(1, D_MODEL), 0, NCTX).reshape(BATCH, SEQ, D_MODEL)
    y_sample = _final_norm(x, final_g.reshape(1, D_MODEL), NCTX, NLAT).reshape(DEC_BATCH, DEC_SEQ, D_MODEL)
    return (y_prompt, y_sample, jnp.stack(new_na_k, axis=1), jnp.stack(new_na_v, axis=1),
            jnp.stack(new_gqa_k, axis=1), jnp.stack(new_gqa_v, axis=1), jnp.stack(new_ret, axis=1))
```
